```python
import jax, jax.numpy as jnp
from jax import lax
import numpy as np

D_MODEL = 1024
BATCH = 4
SEQ = 8192
DEPTH = 4
DEC_BATCH = 1
DEC_SEQ = 16384
PAST_LEN = 128

GRID_W = 64
EPS = 1e-6
NEG_INF = -1e30
NA_HEADS = 8
NA_HEAD_DIM = 64
NA_WIN_R = 8
NA_WIN_C = 16
NA_COL_BLOCK = 16
NA_KEY_BAND = NA_COL_BLOCK + NA_WIN_C
NA_WIDTH = NA_HEADS * NA_HEAD_DIM
GLA_HEADS = 4
GLA_DK = 64
GLA_DV = 128
GLA_RANK = 16
GLA_TAU = 16.0
GLA_CHUNK = 64
GLA_KW = GLA_HEADS * GLA_DK
GLA_VW = GLA_HEADS * GLA_DV
MIX_WIDTH = NA_WIDTH + GLA_VW
AB_IN = 3 * NA_WIDTH + 2 * GLA_KW + GLA_VW + 2 * GLA_RANK + GLA_VW
CONV_W = 3
N_MEM = 256
MEM_HEADS = 4
MEM_HEAD_DIM = D_MODEL // MEM_HEADS
N_EXPERTS = 16
D_EXPERT = 2 * D_MODEL
EC_CAPACITY = 2
N_AB = (DEPTH + 1) // 2
N_C = DEPTH // 2

kernel_name = 'hybrid_na_gla_conv_ec_encoder'


def rms_norm(x, g):
    xf = x.astype(jnp.float32)
    y = xf * lax.rsqrt(jnp.mean(xf * xf, axis=-1, keepdims=True) + EPS)
    return (y * g.astype(jnp.float32)).astype(x.dtype)


def neighbourhood_attention(q, k, v, rpb):
    b, n, h, d = q.shape
    rows = n // GRID_W
    kr = min(NA_WIN_R, rows)
    n_cb = GRID_W // NA_COL_BLOCK
    qcol = np.arange(GRID_W).reshape(n_cb, NA_COL_BLOCK)
    band0 = np.clip(np.arange(n_cb) * NA_COL_BLOCK - NA_WIN_C // 2, 0, GRID_W - NA_KEY_BAND)
    kcol = band0[:, None] + np.arange(NA_KEY_BAND)[None, :]
    wstart = np.clip(qcol - NA_WIN_C // 2, 0, GRID_W - NA_WIN_C)
    col_ok = (kcol[:, None, :] >= wstart[:, :, None]) & (kcol[:, None, :] < wstart[:, :, None] + NA_WIN_C)
    dc_idx = np.clip(kcol[:, None, :] - qcol[:, :, None] + NA_WIN_C - 1, 0, 2 * NA_WIN_C - 2)
    kb = k.reshape(b, rows, GRID_W, h, d)[:, :, kcol]
    vb = v.reshape(b, rows, GRID_W, h, d)[:, :, kcol]
    q_rows = jnp.moveaxis(q.reshape(b, rows, n_cb, NA_COL_BLOCK, h, d), 1, 0)
    rpb32 = rpb.astype(jnp.float32)
    mask = jnp.asarray(col_ok)[None, :, :, None, None, :]
    scale = d ** -0.5

    def row_block(args):
        r, qr = args
        rs = jnp.clip(r - kr // 2, 0, rows - kr)
        k_nb = lax.dynamic_slice_in_dim(kb, rs, kr, axis=1)
        v_nb = lax.dynamic_slice_in_dim(vb, rs, kr, axis=1)
        s = jnp.einsum('bjqhd,brjkhd->bjqhrk', qr, k_nb, preferred_element_type=jnp.float32) * scale
        dr_idx = rs + jnp.arange(kr) - r + NA_WIN_R - 1
        bias = rpb32[:, dr_idx][:, :, dc_idx]
        s = s + jnp.transpose(bias, (2, 3, 0, 1, 4))[None]
        s = jnp.where(mask, s, NEG_INF)
        p = jax.nn.softmax(s.reshape(s.shape[:4] + (kr * NA_KEY_BAND,)), axis=-1).reshape(s.shape)
        return jnp.einsum('bjqhrk,brjkhd->bjqhd', p.astype(v.dtype), v_nb)

    o = lax.map(row_block, (jnp.arange(rows), q_rows))
    return jnp.moveaxis(o, 0, 1).reshape(b, n, h, d)


def gla_scan(q, k, v, g):
    b, h, n, dk = q.shape
    dv = v.shape[-1]
    c = GLA_CHUNK
    nc = n // c
    q = q.reshape(b, h, nc, c, dk)
    k = k.reshape(b, h, nc, c, dk)
    v = v.reshape(b, h, nc, c, dv)
    bcum = jnp.cumsum(g.reshape(b, h, nc, c, dk), axis=3)
    q_t = q * jnp.exp(bcum)
    k_t = k * jnp.exp(-bcum)
    a = jnp.einsum('bhnik,bhnjk->bhnij', q_t, k_t)
    a = jnp.where(np.tril(np.ones((c, c), dtype=bool)), a, 0.0)
    o_intra = jnp.einsum('bhnij,bhnjv->bhniv', a, v)
    b_last = bcum[:, :, :, -1:, :]
    ds = jnp.einsum('bhnjk,bhnjv->bhnkv', k * jnp.exp(b_last - bcum), v)
    decay = jnp.exp(b_last[:, :, :, 0, :])

    def step(s, inp):
        ds_c, dec_c = inp
        return s * dec_c[..., None] + ds_c, s

    s0 = jnp.zeros((b, h, dk, dv), jnp.float32)
    _, s_prev = lax.scan(step, s0, (jnp.moveaxis(ds, 2, 0), jnp.moveaxis(decay, 2, 0)))
    o_inter = jnp.einsum('bhnik,bhnkv->bhniv', q_t, jnp.moveaxis(s_prev, 0, 2))
    return (o_intra + o_inter).reshape(b, h, n, dv)


def mixer_ab(h, i, p):
    b, n, _ = h.shape
    z = h @ p['w_in_ab'][i]
    sizes = (NA_WIDTH, NA_WIDTH, NA_WIDTH, GLA_KW, GLA_KW, GLA_VW, GLA_RANK, GLA_RANK, GLA_VW)
    offs = [int(o) for o in np.cumsum(sizes)[:-1]]
    qa, ka, va, qb, kb, vb, gf, gb, r = jnp.split(z, offs, axis=-1)
    o_a = neighbourhood_attention(qa.reshape(b, n, NA_HEADS, NA_HEAD_DIM), ka.reshape(b, n, NA_HEADS, NA_HEAD_DIM),
                                  va.reshape(b, n, NA_HEADS, NA_HEAD_DIM), p['na_rpb'][i]).reshape(b, n, NA_WIDTH)
    def to_bhn(t):
        return jnp.transpose(t.astype(jnp.float32).reshape(b, n, GLA_HEADS, -1), (0, 2, 1, 3))
    q = to_bhn(qb) * GLA_DK ** -0.5
    k = to_bhn(kb)
    v = to_bhn(vb)
    g_f = to_bhn(jax.nn.log_sigmoid(gf.astype(jnp.float32) @ p['gla_gw_f'][i].astype(jnp.float32) + p['gla_gb_f'][i].astype(jnp.float32)) / GLA_TAU)
    g_b = to_bhn(jax.nn.log_sigmoid(gb.astype(jnp.float32) @ p['gla_gw_b'][i].astype(jnp.float32) + p['gla_gb_b'][i].astype(jnp.float32)) / GLA_TAU)
    o_fw = gla_scan(q, k, v, g_f)
    o_bw = jnp.flip(gla_scan(jnp.flip(q, 2), jnp.flip(k, 2), jnp.flip(v, 2), jnp.flip(g_b, 2)), 2)
    o = jnp.transpose(o_fw + o_bw, (0, 2, 1, 3))
    o = rms_norm(o, p['gla_norm'][i]) * jax.nn.silu(r.astype(jnp.float32)).reshape(b, n, GLA_HEADS, GLA_DV)
    o_b = o.reshape(b, n, GLA_VW).astype(h.dtype)
    return jnp.concatenate([o_a, o_b], axis=-1) @ p['w_out_ab'][i]


def mixer_c(h, i, p):
    bg, cg, xt = jnp.split(h @ p['w_in_c'][i], 3, axis=-1)
    u = cg * xt
    kern = p['conv_w'][i][:, None, :].astype(u.dtype)
    conv = lax.conv_general_dilated(u, kern, window_strides=(1,), padding=((CONV_W // 2, CONV_W // 2),),
                                    dimension_numbers=('NWC', 'WIO', 'NWC'), feature_group_count=u.shape[-1])
    return (bg * conv) @ p['w_out_c'][i]


def memory_attention(h, mem, layer, p):
    b, n, dm = h.shape
    q = (h @ p['w_mq'][layer]).reshape(b, n, MEM_HEADS, MEM_HEAD_DIM)
    k = (mem @ p['w_mk'][layer]).reshape(b, -1, MEM_HEADS, MEM_HEAD_DIM)
    v = (mem @ p['w_mv'][layer]).reshape(b, -1, MEM_HEADS, MEM_HEAD_DIM)
    s = jnp.einsum('bqhd,bkhd->bhqk', q, k, preferred_element_type=jnp.float32) * MEM_HEAD_DIM ** -0.5
    a = jax.nn.softmax(s, axis=-1).astype(v.dtype)
    o = jnp.einsum('bhqk,bkhd->bqhd', a, v).reshape(b, n, dm)
    return o @ p['w_mo'][layer]


def expert_choice_ffn(h, layer, p):
    b, n, dm = h.shape
    t = b * n
    flat = h.reshape(t, dm)
    aff = jax.nn.softmax(flat.astype(jnp.float32) @ p['router'][layer].astype(jnp.float32), axis=-1)
    cap = EC_CAPACITY * t // N_EXPERTS
    gate, idx = lax.top_k(aff.T, cap)
    xe = flat[idx]
    hid = jax.nn.silu(jnp.einsum('ecd,edf->ecf', xe, p['w_gate'][layer])) * jnp.einsum('ecd,edf->ecf', xe, p['w_up'][layer])
    ye = jnp.einsum('ecf,efd->ecd', hid, p['w_down'][layer]) * gate[..., None].astype(hid.dtype)
    out = jnp.zeros_like(flat).at[idx.reshape(-1)].add(ye.reshape(-1, dm).astype(flat.dtype))
    return out.reshape(b, n, dm)


def trunk(x, mem, p):
    for layer in range(DEPTH):
        hn = rms_norm(x, p['norm_mix'][layer])
        if layer % 2 == 0:
            x = x + mixer_ab(hn, layer // 2, p)
        else:
            x = x + mixer_c(hn, layer // 2, p)
        x = x + memory_attention(rms_norm(x, p['norm_mem'][layer]), mem, layer, p)
        x = x + expert_choice_ffn(rms_norm(x, p['norm_ffn'][layer]), layer, p)
    return rms_norm(x, p['final_norm'])


def setup_inputs(seed: int = 0) -> dict:
    key = jax.random.key(seed)
    ks = jax.random.split(key, 27)
    f32 = jnp.float32

    def w(k, shape, fan_in):
        return jax.random.normal(k, shape, f32) * (fan_in ** -0.5)

    def gain(k, shape):
        return 1.0 + 0.02 * jax.random.normal(k, shape, f32)

    return {
        'x_prompt': jax.random.normal(ks[0], (BATCH, SEQ, D_MODEL), f32),
        'x_sample': jax.random.normal(ks[1], (DEC_BATCH, DEC_SEQ, D_MODEL), f32),
        'mem_prompt': jax.random.normal(ks[2], (BATCH, N_MEM, D_MODEL), f32),
        'mem_sample': jax.random.normal(ks[3], (DEC_BATCH, N_MEM, D_MODEL), f32),
        'w_in_ab': w(ks[4], (N_AB, D_MODEL, AB_IN), D_MODEL),
        'na_rpb': 0.1 * jax.random.normal(ks[5], (N_AB, NA_HEADS, 2 * NA_WIN_R - 1, 2 * NA_WIN_C - 1), f32),
        'gla_gw_f': w(ks[6], (N_AB, GLA_RANK, GLA_KW), GLA_RANK),
        'gla_gb_f': 0.1 * jax.random.normal(ks[7], (N_AB, GLA_KW), f32),
        'gla_gw_b': w(ks[8], (N_AB, GLA_RANK, GLA_KW), GLA_RANK),
        'gla_gb_b': 0.1 * jax.random.normal(ks[9], (N_AB, GLA_KW), f32),
        'gla_norm': gain(ks[10], (N_AB, GLA_DV)),
        'w_out_ab': w(ks[11], (N_AB, MIX_WIDTH, D_MODEL), MIX_WIDTH),
        'w_in_c': w(ks[12], (N_C, D_MODEL, 3 * D_MODEL), D_MODEL),
        'conv_w': w(ks[13], (N_C, CONV_W, D_MODEL), CONV_W),
        'w_out_c': w(ks[14], (N_C, D_MODEL, D_MODEL), D_MODEL),
        'norm_mix': gain(ks[15], (DEPTH, D_MODEL)),
        'norm_mem': gain(ks[16], (DEPTH, D_MODEL)),
        'norm_ffn': gain(ks[17], (DEPTH, D_MODEL)),
        'w_mq': w(ks[18], (DEPTH, D_MODEL, D_MODEL), D_MODEL),
        'w_mk': w(ks[19], (DEPTH, D_MODEL, D_MODEL), D_MODEL),
        'w_mv': w(ks[20], (DEPTH, D_MODEL, D_MODEL), D_MODEL),
        'w_mo': w(ks[21], (DEPTH, D_MODEL, D_MODEL), D_MODEL),
        'router': w(ks[22], (DEPTH, D_MODEL, N_EXPERTS), D_MODEL),
        'w_gate': w(ks[23], (DEPTH, N_EXPERTS, D_MODEL, D_EXPERT), D_MODEL),
        'w_up': w(ks[24], (DEPTH, N_EXPERTS, D_MODEL, D_EXPERT), D_MODEL),
        'w_down': w(ks[25], (DEPTH, N_EXPERTS, D_EXPERT, D_MODEL), D_EXPERT),
        'final_norm': gain(ks[26], (D_MODEL,)),
    }


def reference(x_prompt, x_sample, mem_prompt, mem_sample, w_in_ab, na_rpb, gla_gw_f, gla_gb_f, gla_gw_b,
              gla_gb_b, gla_norm, w_out_ab, w_in_c, conv_w, w_out_c, norm_mix, norm_mem, norm_ffn,
              w_mq, w_mk, w_mv, w_mo, router, w_gate, w_up, w_down, final_norm):
    p = dict(w_in_ab=w_in_ab, na_rpb=na_rpb, gla_gw_f=gla_gw_f, gla_gb_f=gla_gb_f, gla_gw_b=gla_gw_b,
             gla_gb_b=gla_gb_b, gla_norm=gla_norm, w_out_ab=w_out_ab, w_in_c=w_in_c, conv_w=conv_w,
             w_out_c=w_out_c, norm_mix=norm_mix, norm_mem=norm_mem, norm_ffn=norm_ffn, w_mq=w_mq,
             w_mk=w_mk, w_mv=w_mv, w_mo=w_mo, router=router, w_gate=w_gate, w_up=w_up, w_down=w_down,
             final_norm=final_norm)
    y_prompt = trunk(x_prompt, mem_prompt, p)
    y_sample = trunk(x_sample, mem_sample, p)
    return (y_prompt, y_sample)
```

```python
import functools

import jax
import jax.numpy as jnp
import numpy as np
from jax import lax
from jax.experimental import pallas as pl
from jax.experimental.pallas import tpu as pltpu

D_MODEL = 1024
DEPTH = 4
GRID_W = 64
EPS = 1e-6
NEG_INF = -1e30
NA_HEADS = 8
NA_HEAD_DIM = 64
NA_WIN_R = 8
NA_WIN_C = 16
NA_COL_BLOCK = 16
NA_KEY_BAND = NA_COL_BLOCK + NA_WIN_C
NA_WIDTH = NA_HEADS * NA_HEAD_DIM
GLA_HEADS = 4
GLA_DK = 64
GLA_DV = 128
GLA_RANK = 16
GLA_TAU = 16.0
GLA_CHUNK = 64
GLA_KW = GLA_HEADS * GLA_DK
GLA_VW = GLA_HEADS * GLA_DV
CONV_W = 3
MEM_HEADS = 4
MEM_HEAD_DIM = D_MODEL // MEM_HEADS
N_EXPERTS = 16
D_EXPERT = 2 * D_MODEL
EC_CAPACITY = 2

VMEM_LIMIT_BYTES = 48 * 1024 * 1024
ROW_TILE = 512


def _rms(x, g):
    xf = x.astype(jnp.float32)
    return xf * lax.rsqrt(jnp.mean(xf * xf, axis=-1, keepdims=True) + EPS) * g


def _mm_body(*refs, has_norm, has_res):
    x_ref, w_ref = refs[0], refs[1]
    pos = 2
    x = x_ref[...]
    if has_norm:
        x = _rms(x, refs[pos][...])
        pos += 1
    acc = jnp.dot(x.astype(jnp.bfloat16), w_ref[...], preferred_element_type=jnp.float32)
    if has_res:
        acc = acc + refs[pos][...]
        pos += 1
    o_ref = refs[pos]
    o_ref[...] = acc.astype(o_ref.dtype)


def matmul(x, w, *, gain=None, residual=None, out_dtype=jnp.float32, tm=ROW_TILE):
    m, k = x.shape
    n = w.shape[1]
    tm = min(tm, m)
    in_specs = [pl.BlockSpec((tm, k), lambda i: (i, 0)), pl.BlockSpec((k, n), lambda i: (0, 0))]
    args = [x, w]
    if gain is not None:
        in_specs.append(pl.BlockSpec((1, k), lambda i: (0, 0)))
        args.append(gain.reshape(1, k).astype(jnp.float32))
    if residual is not None:
        in_specs.append(pl.BlockSpec((tm, n), lambda i: (i, 0)))
        args.append(residual)
    return pl.pallas_call(
        functools.partial(_mm_body, has_norm=gain is not None, has_res=residual is not None),
        grid=(m // tm,),
        in_specs=in_specs,
        out_specs=pl.BlockSpec((tm, n), lambda i: (i, 0)),
        out_shape=jax.ShapeDtypeStruct((m, n), out_dtype),
        compiler_params=pltpu.CompilerParams(
            dimension_semantics=("parallel",), vmem_limit_bytes=VMEM_LIMIT_BYTES),
        name="row_matmul",
    )(*args)


def _ffn_body(xe_ref, gate_ref, wg_ref, wu_ref, wd_ref, o_ref):
    xe = xe_ref[0]
    a = jnp.dot(xe, wg_ref[0], preferred_element_type=jnp.float32)
    b = jnp.dot(xe, wu_ref[0], preferred_element_type=jnp.float32)
    hid = (a * jax.nn.sigmoid(a) * b).astype(jnp.bfloat16)
    y = jnp.dot(hid, wd_ref[0], preferred_element_type=jnp.float32)
    o_ref[0] = y * gate_ref[0]


def expert_ffn(xe, gate, wg, wu, wd, *, tm=256):
    e, cap, d = xe.shape
    f = wg.shape[-1]
    return pl.pallas_call(
        _ffn_body,
        grid=(e, cap // tm),
        in_specs=[
            pl.BlockSpec((1, tm, d), lambda i, j: (i, j, 0)),
            pl.BlockSpec((1, tm, 1), lambda i, j: (i, j, 0)),
            pl.BlockSpec((1, d, f), lambda i, j: (i, 0, 0)),
            pl.BlockSpec((1, d, f), lambda i, j: (i, 0, 0)),
            pl.BlockSpec((1, f, d), lambda i, j: (i, 0, 0)),
        ],
        out_specs=pl.BlockSpec((1, tm, d), lambda i, j: (i, j, 0)),
        out_shape=jax.ShapeDtypeStruct((e, cap, d), jnp.float32),
        compiler_params=pltpu.CompilerParams(
            dimension_semantics=("parallel", "parallel"), vmem_limit_bytes=VMEM_LIMIT_BYTES),
        name="expert_ffn",
    )(xe, gate, wg, wu, wd)


def rms_norm(x, g):
    xf = x.astype(jnp.float32)
    y = xf * lax.rsqrt(jnp.mean(xf * xf, axis=-1, keepdims=True) + EPS)
    return (y * g.astype(jnp.float32)).astype(x.dtype)


def neighbourhood_attention(q, k, v, rpb):
    b, n, h, d = q.shape
    rows = n // GRID_W
    kr = min(NA_WIN_R, rows)
    n_cb = GRID_W // NA_COL_BLOCK
    qcol = np.arange(GRID_W).reshape(n_cb, NA_COL_BLOCK)
    band0 = np.clip(np.arange(n_cb) * NA_COL_BLOCK - NA_WIN_C // 2, 0, GRID_W - NA_KEY_BAND)
    kcol = band0[:, None] + np.arange(NA_KEY_BAND)[None, :]
    wstart = np.clip(qcol - NA_WIN_C // 2, 0, GRID_W - NA_WIN_C)
    col_ok = (kcol[:, None, :] >= wstart[:, :, None]) & (kcol[:, None, :] < wstart[:, :, None] + NA_WIN_C)
    dc_idx = np.clip(kcol[:, None, :] - qcol[:, :, None] + NA_WIN_C - 1, 0, 2 * NA_WIN_C - 2)
    kb = k.reshape(b, rows, GRID_W, h, d)[:, :, kcol]
    vb = v.reshape(b, rows, GRID_W, h, d)[:, :, kcol]
    q_rows = jnp.moveaxis(q.reshape(b, rows, n_cb, NA_COL_BLOCK, h, d), 1, 0)
    rpb32 = rpb.astype(jnp.float32)
    mask = jnp.asarray(col_ok)[None, :, :, None, None, :]
    scale = d ** -0.5

    def row_block(args):
        r, qr = args
        rs = jnp.clip(r - kr // 2, 0, rows - kr)
        k_nb = lax.dynamic_slice_in_dim(kb, rs, kr, axis=1)
        v_nb = lax.dynamic_slice_in_dim(vb, rs, kr, axis=1)
        s = jnp.einsum('bjqhd,brjkhd->bjqhrk', qr, k_nb, preferred_element_type=jnp.float32) * scale
        dr_idx = rs + jnp.arange(kr) - r + NA_WIN_R - 1
        bias = rpb32[:, dr_idx][:, :, dc_idx]
        s = s + jnp.transpose(bias, (2, 3, 0, 1, 4))[None]
        s = jnp.where(mask, s, NEG_INF)
        p = jax.nn.softmax(s.reshape(s.shape[:4] + (kr * NA_KEY_BAND,)), axis=-1).reshape(s.shape)
        return jnp.einsum('bjqhrk,brjkhd->bjqhd', p.astype(v.dtype), v_nb)

    o = lax.map(row_block, (jnp.arange(rows), q_rows))
    return jnp.moveaxis(o, 0, 1).reshape(b, n, h, d)


def gla_scan(q, k, v, g):
    b, h, n, dk = q.shape
    dv = v.shape[-1]
    c = GLA_CHUNK
    nc = n // c
    q = q.reshape(b, h, nc, c, dk)
    k = k.reshape(b, h, nc, c, dk)
    v = v.reshape(b, h, nc, c, dv)
    bcum = jnp.cumsum(g.reshape(b, h, nc, c, dk), axis=3)
    q_t = q * jnp.exp(bcum)
    k_t = k * jnp.exp(-bcum)
    a = jnp.einsum('bhnik,bhnjk->bhnij', q_t, k_t)
    a = jnp.where(np.tril(np.ones((c, c), dtype=bool)), a, 0.0)
    o_intra = jnp.einsum('bhnij,bhnjv->bhniv', a, v)
    b_last = bcum[:, :, :, -1:, :]
    ds = jnp.einsum('bhnjk,bhnjv->bhnkv', k * jnp.exp(b_last - bcum), v)
    decay = jnp.exp(b_last[:, :, :, 0, :])

    def step(s, inp):
        ds_c, dec_c = inp
        return s * dec_c[..., None] + ds_c, s

    s0 = jnp.zeros((b, h, dk, dv), jnp.float32)
    _, s_prev = lax.scan(step, s0, (jnp.moveaxis(ds, 2, 0), jnp.moveaxis(decay, 2, 0)))
    o_inter = jnp.einsum('bhnik,bhnkv->bhniv', q_t, jnp.moveaxis(s_prev, 0, 2))
    return (o_intra + o_inter).reshape(b, h, n, dv)


def mixer_ab(x, i, p, layer):
    b, n, _ = x.shape
    t = b * n
    z = matmul(x.reshape(t, D_MODEL), p['w_in_ab'][i], gain=p['norm_mix'][layer]).reshape(b, n, -1)
    sizes = (NA_WIDTH, NA_WIDTH, NA_WIDTH, GLA_KW, GLA_KW, GLA_VW, GLA_RANK, GLA_RANK, GLA_VW)
    offs = [int(o) for o in np.cumsum(sizes)[:-1]]
    qa, ka, va, qb, kb, vb, gf, gb, r = jnp.split(z, offs, axis=-1)
    o_a = neighbourhood_attention(
        qa.reshape(b, n, NA_HEADS, NA_HEAD_DIM), ka.reshape(b, n, NA_HEADS, NA_HEAD_DIM),
        va.reshape(b, n, NA_HEADS, NA_HEAD_DIM), p['na_rpb'][i]).reshape(b, n, NA_WIDTH)

    def to_bhn(a):
        return jnp.transpose(a.astype(jnp.float32).reshape(b, n, GLA_HEADS, -1), (0, 2, 1, 3))

    q = to_bhn(qb) * GLA_DK ** -0.5
    k = to_bhn(kb)
    v = to_bhn(vb)
    g_f = to_bhn(jax.nn.log_sigmoid(gf @ p['gla_gw_f'][i] + p['gla_gb_f'][i]) / GLA_TAU)
    g_b = to_bhn(jax.nn.log_sigmoid(gb @ p['gla_gw_b'][i] + p['gla_gb_b'][i]) / GLA_TAU)
    o_fw = gla_scan(q, k, v, g_f)
    o_bw = jnp.flip(gla_scan(jnp.flip(q, 2), jnp.flip(k, 2), jnp.flip(v, 2), jnp.flip(g_b, 2)), 2)
    o = jnp.transpose(o_fw + o_bw, (0, 2, 1, 3))
    o = rms_norm(o, p['gla_norm'][i]) * jax.nn.silu(r).reshape(b, n, GLA_HEADS, GLA_DV)
    o_b = o.reshape(b, n, GLA_VW)
    cat = jnp.concatenate([o_a, o_b], axis=-1).reshape(t, -1)
    return matmul(cat, p['w_out_ab'][i], residual=x.reshape(t, D_MODEL)).reshape(b, n, D_MODEL)


def mixer_c(x, i, p, layer):
    b, n, _ = x.shape
    t = b * n
    z = matmul(x.reshape(t, D_MODEL), p['w_in_c'][i], gain=p['norm_mix'][layer]).reshape(b, n, -1)
    bg, cg, xt = jnp.split(z, 3, axis=-1)
    u = cg * xt
    cw = p['conv_w'][i]
    zero = jnp.zeros_like(u[:, :1])
    conv = (cw[0] * jnp.concatenate([zero, u[:, :-1]], axis=1) + cw[1] * u
            + cw[2] * jnp.concatenate([u[:, 1:], zero], axis=1))
    return matmul((bg * conv).reshape(t, D_MODEL), p['w_out_c'][i],
                  residual=x.reshape(t, D_MODEL)).reshape(b, n, D_MODEL)


def memory_attention(x, mem, layer, p):
    b, n, dm = x.shape
    t = b * n
    q = matmul(x.reshape(t, dm), p['w_mq'][layer], gain=p['norm_mem'][layer])
    q = q.reshape(b, n, MEM_HEADS, MEM_HEAD_DIM)
    memf = mem.reshape(-1, dm)
    k = matmul(memf, p['w_mk'][layer]).reshape(b, -1, MEM_HEADS, MEM_HEAD_DIM)
    v = matmul(memf, p['w_mv'][layer]).reshape(b, -1, MEM_HEADS, MEM_HEAD_DIM)
    s = jnp.einsum('bqhd,bkhd->bhqk', q, k, preferred_element_type=jnp.float32) * MEM_HEAD_DIM ** -0.5
    a = jax.nn.softmax(s, axis=-1).astype(v.dtype)
    o = jnp.einsum('bhqk,bkhd->bqhd', a, v).reshape(t, dm)
    return matmul(o, p['w_mo'][layer], residual=x.reshape(t, dm)).reshape(b, n, dm)


def expert_choice_ffn(x, layer, p):
    b, n, dm = x.shape
    t = b * n
    flat = rms_norm(x.reshape(t, dm), p['norm_ffn'][layer])
    aff = jax.nn.softmax(
        jnp.dot(flat, p['router'][layer], precision=lax.Precision.HIGHEST), axis=-1)
    cap = EC_CAPACITY * t // N_EXPERTS
    gate, idx = lax.top_k(aff.T, cap)
    xe = flat.astype(jnp.bfloat16)[idx]
    ye = expert_ffn(xe, gate[..., None], p['w_gate'][layer], p['w_up'][layer], p['w_down'][layer])
    out = x.reshape(t, dm).at[idx.reshape(-1)].add(ye.reshape(-1, dm))
    return out.reshape(b, n, dm)


def trunk(x, mem, p):
    for layer in range(DEPTH):
        if layer % 2 == 0:
            x = mixer_ab(x, layer // 2, p, layer)
        else:
            x = mixer_c(x, layer // 2, p, layer)
        x = memory_attention(x, mem, layer, p)
        x = expert_choice_ffn(x, layer, p)
    return rms_norm(x, p['final_norm'])


def kernel(x_prompt, x_sample, mem_prompt, mem_sample, w_in_ab, na_rpb, gla_gw_f, gla_gb_f, gla_gw_b,
           gla_gb_b, gla_norm, w_out_ab, w_in_c, conv_w, w_out_c, norm_mix, norm_mem, norm_ffn,
           w_mq, w_mk, w_mv, w_mo, router, w_gate, w_up, w_down, final_norm):
    bf = jnp.bfloat16
    p = dict(w_in_ab=w_in_ab.astype(bf), na_rpb=na_rpb, gla_gw_f=gla_gw_f, gla_gb_f=gla_gb_f,
             gla_gw_b=gla_gw_b, gla_gb_b=gla_gb_b, gla_norm=gla_norm, w_out_ab=w_out_ab.astype(bf),
             w_in_c=w_in_c.astype(bf), conv_w=conv_w, w_out_c=w_out_c.astype(bf), norm_mix=norm_mix,
             norm_mem=norm_mem, norm_ffn=norm_ffn, w_mq=w_mq.astype(bf), w_mk=w_mk.astype(bf),
             w_mv=w_mv.astype(bf), w_mo=w_mo.astype(bf), router=router, w_gate=w_gate.astype(bf),
             w_up=w_up.astype(bf), w_down=w_down.astype(bf), final_norm=final_norm)
    y_prompt = trunk(x_prompt, mem_prompt, p)
    y_sample = trunk(x_sample, mem_sample, p)
    return (y_prompt, y_sample)
```

```python
import functools

import jax
import jax.numpy as jnp
import numpy as np
from jax import lax
from jax.experimental import pallas as pl
from jax.experimental.pallas import tpu as pltpu

D_MODEL = 1024
DEPTH = 4
GRID_W = 64
EPS = 1e-6
NEG_INF = -1e30
NA_HEADS = 8
NA_HEAD_DIM = 64
NA_WIN_R = 8
NA_WIN_C = 16
NA_WIDTH = NA_HEADS * NA_HEAD_DIM
GLA_HEADS = 4
GLA_DK = 64
GLA_DV = 128
GLA_RANK = 16
GLA_TAU = 16.0
GLA_CHUNK = 64
GLA_KW = GLA_HEADS * GLA_DK
GLA_VW = GLA_HEADS * GLA_DV
CONV_W = 3
MEM_HEADS = 4
MEM_HEAD_DIM = D_MODEL // MEM_HEADS
N_EXPERTS = 16
D_EXPERT = 2 * D_MODEL
EC_CAPACITY = 2

VMEM_LIMIT_BYTES = 48 * 1024 * 1024
ROW_TILE = 512
SUBLANES = 8
NA_ROWS_PER_BLOCK = 4
NA_BLOCK_TOKENS = NA_ROWS_PER_BLOCK * GRID_W
GLA_BLOCK = 512
BF16 = jnp.bfloat16
F32 = jnp.float32


def _params(*sem):
    return pltpu.CompilerParams(dimension_semantics=sem, vmem_limit_bytes=VMEM_LIMIT_BYTES)


def _rms(x, g):
    return x * lax.rsqrt(jnp.mean(x * x, axis=-1, keepdims=True) + EPS) * g


def _row_spec(tm, n, col=0):
    return pl.BlockSpec((tm, n), lambda i: (i, col))


def _full_spec(shape):
    return pl.BlockSpec(shape, lambda *_: (0,) * len(shape))


def _mm_body(x_ref, w_ref, o_ref):
    o_ref[...] = jnp.dot(x_ref[...].astype(BF16), w_ref[...],
                         preferred_element_type=F32).astype(o_ref.dtype)


def matmul(x, w, out_dtype):
    m, k = x.shape
    n = w.shape[1]
    tm = min(ROW_TILE, m)
    return pl.pallas_call(
        _mm_body, grid=(m // tm,),
        in_specs=[_row_spec(tm, k), _full_spec((k, n))],
        out_specs=_row_spec(tm, n),
        out_shape=jax.ShapeDtypeStruct((m, n), out_dtype),
        compiler_params=_params("parallel"), name="row_matmul",
    )(x, w)


def _log_sigmoid(x):
    return jnp.minimum(x, 0.0) - jnp.log1p(jnp.exp(-jnp.abs(x)))


def _in_ab_body(x_ref, g_ref, wa_ref, wb_ref, wg_ref, gwf_ref, gbf_ref, gwb_ref, gbb_ref,
                za_ref, zb_ref):
    hn = _rms(x_ref[...], g_ref[...]).astype(BF16)
    za_ref[...] = jnp.dot(hn, wa_ref[...], preferred_element_type=F32).astype(za_ref.dtype)
    nb = wb_ref.shape[1]
    zb_ref[:, :nb] = jnp.dot(hn, wb_ref[...], preferred_element_type=F32)
    lowrank = jnp.dot(hn, wg_ref[...], preferred_element_type=F32)
    pre_f = jnp.dot(lowrank[:, :GLA_RANK], gwf_ref[...], preferred_element_type=F32,
                    precision=lax.Precision.HIGHEST) + gbf_ref[...]
    pre_b = jnp.dot(lowrank[:, GLA_RANK:], gwb_ref[...], preferred_element_type=F32,
                    precision=lax.Precision.HIGHEST) + gbb_ref[...]
    zb_ref[:, nb:nb + GLA_KW] = _log_sigmoid(pre_f) / GLA_TAU
    zb_ref[:, nb + GLA_KW:] = _log_sigmoid(pre_b) / GLA_TAU


def in_proj_ab(x, gain, w_in, gwf, gbf, gwb, gbb):
    t = x.shape[0]
    tm = ROW_TILE
    na_w = 3 * NA_WIDTH
    gl_w = 2 * GLA_KW + 2 * GLA_VW
    o = np.cumsum([0, NA_WIDTH, NA_WIDTH, NA_WIDTH, GLA_KW, GLA_KW, GLA_VW, GLA_RANK, GLA_RANK, GLA_VW])
    wa = w_in[:, :o[3]].astype(BF16)
    wb = jnp.concatenate([w_in[:, o[3]:o[6]], w_in[:, o[8]:o[9]]], axis=1).astype(BF16)
    wg = w_in[:, o[6]:o[8]].astype(BF16)
    return pl.pallas_call(
        _in_ab_body, grid=(t // tm,),
        in_specs=[_row_spec(tm, D_MODEL), _full_spec((1, D_MODEL)), _full_spec((D_MODEL, na_w)),
                  _full_spec((D_MODEL, gl_w)), _full_spec((D_MODEL, 2 * GLA_RANK)),
                  _full_spec((GLA_RANK, GLA_KW)), _full_spec((1, GLA_KW)),
                  _full_spec((GLA_RANK, GLA_KW)), _full_spec((1, GLA_KW))],
        out_specs=[_row_spec(tm, na_w), _row_spec(tm, gl_w + 2 * GLA_KW)],
        out_shape=[jax.ShapeDtypeStruct((t, na_w), BF16),
                   jax.ShapeDtypeStruct((t, gl_w + 2 * GLA_KW), F32)],
        compiler_params=_params("parallel"), name="in_proj_ab",
    )(x, gain.reshape(1, -1), wa, wb, wg, gwf, gbf.reshape(1, -1), gwb, gbb.reshape(1, -1))


def na_bias_table(rpb):
    rb = NA_ROWS_PER_BLOCK
    a = np.arange(rb)[:, None, None, None]
    c = np.arange(GRID_W)[None, :, None, None]
    u = np.arange(3 * rb)[None, None, :, None]
    kc = np.arange(GRID_W)[None, None, None, :]
    wstart = np.clip(c - NA_WIN_C // 2, 0, GRID_W - NA_WIN_C)
    col_ok = (kc >= wstart) & (kc < wstart + NA_WIN_C)
    dc = np.clip(kc - c + NA_WIN_C - 1, 0, 2 * NA_WIN_C - 2)
    dr = np.clip(u - a + NA_WIN_R // 2 - 1, 0, 2 * NA_WIN_R - 2)
    row_ok = [
        (u >= rb) & (u < rb + NA_WIN_R) & (a >= 0),
        (u - a >= 0) & (u - a < NA_WIN_R),
        (u >= 0) & (u < NA_WIN_R) & (a >= 0),
    ]
    shape = (rb, GRID_W, 3 * rb, GRID_W)
    flat = (NA_BLOCK_TOKENS, 3 * NA_BLOCK_TOKENS)
    dr_b = np.broadcast_to(dr, shape).reshape(flat)
    dc_b = np.broadcast_to(dc, shape).reshape(flat)
    vals = rpb.astype(F32)[:, dr_b, dc_b]
    out = []
    for ok in row_ok:
        m = np.broadcast_to(ok & col_ok, shape).reshape(flat)
        out.append(jnp.where(jnp.asarray(m)[None], vals, NEG_INF))
    return jnp.stack(out)


def _na_body(q_ref, kp_ref, kc_ref, kn_ref, vp_ref, vc_ref, vn_ref, bias_ref, o_ref):
    lane = lax.broadcasted_iota(jnp.int32, (1, 2 * NA_HEAD_DIM), 1)
    scale = NA_HEAD_DIM ** -0.5
    nt = (((1,), (1,)), ((), ()))
    for hp in range(NA_HEADS // 2):
        cols = slice(hp * 2 * NA_HEAD_DIM, (hp + 1) * 2 * NA_HEAD_DIM)
        q = q_ref[:, cols] * scale
        ks = [r[:, cols] for r in (kp_ref, kc_ref, kn_ref)]
        vs = [r[:, cols] for r in (vp_ref, vc_ref, vn_ref)]
        o_pair = None
        for sub in range(2):
            sel = (lane < NA_HEAD_DIM) if sub == 0 else (lane >= NA_HEAD_DIM)
            qm = jnp.where(sel, q, jnp.zeros_like(q))
            s = jnp.concatenate(
                [lax.dot_general(qm, kk, nt, preferred_element_type=F32) for kk in ks], axis=1)
            s = s + bias_ref[0, 2 * hp + sub]
            p = jnp.exp(s - jnp.max(s, axis=-1, keepdims=True))
            l = jnp.sum(p, axis=-1, keepdims=True)
            pb = p.astype(BF16)
            o = None
            for j, vv in enumerate(vs):
                t = jnp.dot(pb[:, j * NA_BLOCK_TOKENS:(j + 1) * NA_BLOCK_TOKENS], vv,
                            preferred_element_type=F32)
                o = t if o is None else o + t
            o = o / l
            o_pair = o if o_pair is None else jnp.where(sel, o, o_pair)
        o_ref[:, cols] = o_pair.astype(o_ref.dtype)


def neighbourhood_attention(za, rpb, batch, n):
    t = za.shape[0]
    bt = NA_BLOCK_TOKENS
    nblk = n // bt
    assert n % bt == 0 and nblk >= 3
    bias = na_bias_table(rpb)

    def qmap(b, i):
        return (b * nblk + i, 0)

    def kmap(d, col):
        return lambda b, i: (b * nblk + jnp.clip(i + d, 0, nblk - 1), col)

    def bmap(b, i):
        return (jnp.where(i == 0, 0, jnp.where(i == nblk - 1, 2, 1)), 0, 0, 0)

    def blk(m):
        return pl.BlockSpec((bt, NA_WIDTH), m)

    return pl.pallas_call(
        _na_body, grid=(batch, nblk),
        in_specs=[blk(qmap), blk(kmap(-1, 1)), blk(kmap(0, 1)), blk(kmap(1, 1)),
                  blk(kmap(-1, 2)), blk(kmap(0, 2)), blk(kmap(1, 2)),
                  pl.BlockSpec((1, NA_HEADS, bt, 3 * bt), bmap)],
        out_specs=blk(qmap),
        out_shape=jax.ShapeDtypeStruct((t, NA_WIDTH), BF16),
        compiler_params=_params("parallel", "parallel"), name="neighbourhood_attention",
    )(za, za, za, za, za, za, za, bias)


def _split_dot(a_bf, x, dims):
    hi = x.astype(BF16)
    lo = (x - hi.astype(F32)).astype(BF16)
    return (lax.dot_general(a_bf, hi, dims, preferred_element_type=F32)
            + lax.dot_general(a_bf, lo, dims, preferred_element_type=F32))


def _split_dot_t(x, a_bf, dims):
    hi = x.astype(BF16)
    lo = (x - hi.astype(F32)).astype(BF16)
    return (lax.dot_general(hi, a_bf, dims, preferred_element_type=F32)
            + lax.dot_general(lo, a_bf, dims, preferred_element_type=F32))


def _gla_body(qf_ref, kf_ref, vf_ref, gf_ref, qb_ref, kb_ref, vb_ref, gb_ref,
              of_ref, ob_ref, sf_ref, sb_ref):
    c = GLA_CHUNK
    nchunk = GLA_BLOCK // c

    @pl.when(pl.program_id(1) == 0)
    def _():
        sf_ref[...] = jnp.zeros_like(sf_ref)
        sb_ref[...] = jnp.zeros_like(sb_ref)

    row = lax.broadcasted_iota(jnp.int32, (c, c), 0)
    col = lax.broadcasted_iota(jnp.int32, (c, c), 1)
    tril = row >= col
    triu = row <= col
    ones_cc = jnp.ones((c, c), BF16)
    ones_cv = jnp.ones((c, GLA_DV), BF16)
    mm = (((1,), (0,)), ((), ()))
    nt = (((1,), (1,)), ((), ()))
    tn = (((0,), (0,)), ((), ()))

    def one_chunk(j, q_ref, k_ref, v_ref, g_ref, o_ref, s_ref, keep):
        rows = pl.ds(pl.multiple_of(j * c, c), c)
        g = g_ref[rows, :]
        cum = _split_dot(keep.astype(BF16), g, mm)
        tot_rows = _split_dot(ones_cc, g, mm)
        tot_cols = _split_dot_t(g, ones_cv, tn)
        q = q_ref[rows, :] * (GLA_DK ** -0.5)
        k = k_ref[rows, :]
        q_t = (q * jnp.exp(cum)).astype(BF16)
        k_t = (k * jnp.exp(-cum)).astype(BF16)
        k_d = (k * jnp.exp(tot_rows - cum)).astype(BF16)
        v = v_ref[rows, :].astype(BF16)
        for h in range(GLA_HEADS):
            kc = slice(h * GLA_DK, (h + 1) * GLA_DK)
            vc = slice(h * GLA_DV, (h + 1) * GLA_DV)
            a = lax.dot_general(q_t[:, kc], k_t[:, kc], nt, preferred_element_type=F32)
            a = jnp.where(keep, a, 0.0).astype(BF16)
            s_prev = s_ref[h]
            o_ref[rows, vc] = (jnp.dot(a, v[:, vc], preferred_element_type=F32)
                               + jnp.dot(q_t[:, kc], s_prev.astype(BF16), preferred_element_type=F32))
            ds = lax.dot_general(k_d[:, kc], v[:, vc], tn, preferred_element_type=F32)
            s_ref[h] = s_prev * jnp.exp(tot_cols[kc, :]) + ds

    def step(j, carry):
        one_chunk(j, qf_ref, kf_ref, vf_ref, gf_ref, of_ref, sf_ref, tril)
        one_chunk(nchunk - 1 - j, qb_ref, kb_ref, vb_ref, gb_ref, ob_ref, sb_ref, triu)
        return carry

    lax.fori_loop(0, nchunk, step, 0)


def gla_bidirectional(zb, batch, n):
    t = zb.shape[0]
    nb = n // GLA_BLOCK
    assert n % GLA_BLOCK == 0

    def fwd(col):
        return lambda b, i: (b * nb + i, col)

    def bwd(col):
        return lambda b, i: (b * nb + nb - 1 - i, col)

    def kw(m):
        return pl.BlockSpec((GLA_BLOCK, GLA_KW), m)

    def vw(m):
        return pl.BlockSpec((GLA_BLOCK, GLA_VW), m)

    return pl.pallas_call(
        _gla_body, grid=(batch, nb),
        in_specs=[kw(fwd(0)), kw(fwd(1)), vw(fwd(1)), kw(fwd(6)),
                  kw(bwd(0)), kw(bwd(1)), vw(bwd(1)), kw(bwd(7))],
        out_specs=[vw(fwd(0)), vw(bwd(0))],
        out_shape=[jax.ShapeDtypeStruct((t, GLA_VW), F32)] * 2,
        scratch_shapes=[pltpu.VMEM((GLA_HEADS, GLA_DK, GLA_DV), F32)] * 2,
        compiler_params=_params("parallel", "arbitrary"), name="gla_bidirectional",
    )(zb, zb, zb, zb, zb, zb, zb, zb)


def _silu(x):
    return x * jax.nn.sigmoid(x)


def _out_ab_body(oa_ref, of_ref, ob_ref, r_ref, gn_ref, wa_ref, wb_ref, x_ref, o_ref):
    o = of_ref[...] + ob_ref[...]
    r = r_ref[...]
    parts = []
    for h in range(GLA_HEADS):
        vc = slice(h * GLA_DV, (h + 1) * GLA_DV)
        parts.append((_rms(o[:, vc], gn_ref[...]) * _silu(r[:, vc])).astype(BF16))
    o_b = jnp.concatenate(parts, axis=1)
    o_ref[...] = (jnp.dot(oa_ref[...], wa_ref[...], preferred_element_type=F32)
                  + jnp.dot(o_b, wb_ref[...], preferred_element_type=F32) + x_ref[...])


def out_proj_ab(o_a, o_fw, o_bw, zb, gla_norm, w_out, x):
    t = x.shape[0]
    tm = ROW_TILE
    w = w_out.astype(BF16)
    return pl.pallas_call(
        _out_ab_body, grid=(t // tm,),
        in_specs=[_row_spec(tm, NA_WIDTH), _row_spec(tm, GLA_VW), _row_spec(tm, GLA_VW),
                  _row_spec(tm, GLA_VW, col=2), _full_spec((1, GLA_DV)),
                  _full_spec((NA_WIDTH, D_MODEL)), _full_spec((GLA_VW, D_MODEL)),
                  _row_spec(tm, D_MODEL)],
        out_specs=_row_spec(tm, D_MODEL),
        out_shape=jax.ShapeDtypeStruct((t, D_MODEL), F32),
        compiler_params=_params("parallel"), name="out_proj_ab",
    )(o_a, o_fw, o_bw, zb, gla_norm.reshape(1, -1), w[:NA_WIDTH], w[NA_WIDTH:], x)


def _in_c_body(x_ref, g_ref, wbg_ref, wcg_ref, wxt_ref, bg_ref, u_ref):
    hn = _rms(x_ref[...], g_ref[...]).astype(BF16)
    bg_ref[...] = jnp.dot(hn, wbg_ref[...], preferred_element_type=F32)
    u_ref[...] = (jnp.dot(hn, wcg_ref[...], preferred_element_type=F32)
                  * jnp.dot(hn, wxt_ref[...], preferred_element_type=F32))


def in_proj_c(x, gain, w_in):
    t = x.shape[0]
    tm = ROW_TILE
    w = w_in.astype(BF16)
    d = D_MODEL
    return pl.pallas_call(
        _in_c_body, grid=(t // tm,),
        in_specs=[_row_spec(tm, d), _full_spec((1, d))] + [_full_spec((d, d))] * 3,
        out_specs=[_row_spec(tm, d)] * 2,
        out_shape=[jax.ShapeDtypeStruct((t, d), F32)] * 2,
        compiler_params=_params("parallel"), name="in_proj_c",
    )(x, gain.reshape(1, -1), w[:, :d], w[:, d:2 * d], w[:, 2 * d:])


def _conv_out_body(u_ref, up_ref, un_ref, bg_ref, cw_ref, w_ref, x_ref, o_ref, *, tiles_per_seq):
    i = pl.program_id(0)
    tm = u_ref.shape[0]
    u = u_ref[...]
    row = lax.broadcasted_iota(jnp.int32, (tm, 1), 0)
    first = (i % tiles_per_seq) == 0
    last = (i % tiles_per_seq) == tiles_per_seq - 1
    prev_row = jnp.where(first, 0.0, up_ref[SUBLANES - 1:SUBLANES, :])
    next_row = jnp.where(last, 0.0, un_ref[0:1, :])
    u_prev = jnp.where(row == 0, prev_row, pltpu.roll(u, 1, axis=0))
    u_next = jnp.where(row == tm - 1, next_row, pltpu.roll(u, tm - 1, axis=0))
    conv = cw_ref[0:1, :] * u_prev + cw_ref[1:2, :] * u + cw_ref[2:3, :] * u_next
    o_ref[...] = (jnp.dot((bg_ref[...] * conv).astype(BF16), w_ref[...], preferred_element_type=F32)
                  + x_ref[...])


def conv_out_proj(u, bg, conv_w, w_out, x, n):
    t = x.shape[0]
    tm = ROW_TILE
    d = D_MODEL
    per = tm // SUBLANES
    nhalo = t // SUBLANES
    assert n % tm == 0
    return pl.pallas_call(
        functools.partial(_conv_out_body, tiles_per_seq=n // tm), grid=(t // tm,),
        in_specs=[_row_spec(tm, d),
                  pl.BlockSpec((SUBLANES, d), lambda i: (jnp.maximum(i * per - 1, 0), 0)),
                  pl.BlockSpec((SUBLANES, d), lambda i: (jnp.minimum((i + 1) * per, nhalo - 1), 0)),
                  _row_spec(tm, d), _full_spec((CONV_W, d)), _full_spec((d, d)), _row_spec(tm, d)],
        out_specs=_row_spec(tm, d),
        out_shape=jax.ShapeDtypeStruct((t, d), F32),
        compiler_params=_params("parallel"), name="conv_out_proj",
    )(u, u, u, bg, conv_w, w_out.astype(BF16), x)


def _mem_body(x_ref, g_ref, wq_ref, k_ref, v_ref, wo_ref, o_ref):
    x = x_ref[...]
    hn = _rms(x, g_ref[...]).astype(BF16)
    q = (jnp.dot(hn, wq_ref[...], preferred_element_type=F32) * MEM_HEAD_DIM ** -0.5).astype(BF16)
    nt = (((1,), (1,)), ((), ()))
    parts = []
    for h in range(MEM_HEADS):
        hc = slice(h * MEM_HEAD_DIM, (h + 1) * MEM_HEAD_DIM)
        s = lax.dot_general(q[:, hc], k_ref[0, :, hc], nt, preferred_element_type=F32)
        p = jnp.exp(s - jnp.max(s, axis=-1, keepdims=True))
        l = jnp.sum(p, axis=-1, keepdims=True)
        o = jnp.dot(p.astype(BF16), v_ref[0, :, hc], preferred_element_type=F32) / l
        parts.append(o.astype(BF16))
    o_ref[...] = jnp.dot(jnp.concatenate(parts, axis=1), wo_ref[...], preferred_element_type=F32) + x


def memory_attention(x, mem, gain, wq, wk, wv, wo, n):
    t, d = x.shape
    b, nm, _ = mem.shape
    tm = ROW_TILE
    per = n // tm
    memf = mem.reshape(b * nm, d)
    k = matmul(memf, wk.astype(BF16), BF16).reshape(b, nm, d)
    v = matmul(memf, wv.astype(BF16), BF16).reshape(b, nm, d)
    kv_spec = pl.BlockSpec((1, nm, d), lambda i: (i // per, 0, 0))
    return pl.pallas_call(
        _mem_body, grid=(t // tm,),
        in_specs=[_row_spec(tm, d), _full_spec((1, d)), _full_spec((d, d)), kv_spec, kv_spec,
                  _full_spec((d, d))],
        out_specs=_row_spec(tm, d),
        out_shape=jax.ShapeDtypeStruct((t, d), F32),
        compiler_params=_params("parallel"), name="memory_attention",
    )(x, gain.reshape(1, -1), wq.astype(BF16), k, v, wo.astype(BF16))


def _ffn_body(xe_ref, gate_ref, wg_ref, wu_ref, wd_ref, o_ref):
    xe = xe_ref[0]
    a = jnp.dot(xe, wg_ref[0], preferred_element_type=F32)
    b = jnp.dot(xe, wu_ref[0], preferred_element_type=F32)
    hid = (_silu(a) * b).astype(BF16)
    o_ref[0] = jnp.dot(hid, wd_ref[0], preferred_element_type=F32) * gate_ref[0]


def expert_ffn(xe, gate, wg, wu, wd, *, tm=256):
    e, cap, d = xe.shape
    f = wg.shape[-1]
    return pl.pallas_call(
        _ffn_body, grid=(e, cap // tm),
        in_specs=[
            pl.BlockSpec((1, tm, d), lambda i, j: (i, j, 0)),
            pl.BlockSpec((1, tm, 1), lambda i, j: (i, j, 0)),
            pl.BlockSpec((1, d, f), lambda i, j: (i, 0, 0)),
            pl.BlockSpec((1, d, f), lambda i, j: (i, 0, 0)),
            pl.BlockSpec((1, f, d), lambda i, j: (i, 0, 0)),
        ],
        out_specs=pl.BlockSpec((1, tm, d), lambda i, j: (i, j, 0)),
        out_shape=jax.ShapeDtypeStruct((e, cap, d), F32),
        compiler_params=_params("parallel", "parallel"), name="expert_ffn",
    )(xe, gate, wg, wu, wd)


def _norm_body(x_ref, g_ref, o_ref):
    o_ref[...] = _rms(x_ref[...], g_ref[...]).astype(o_ref.dtype)


def rms_norm_rows(x, gain, out_dtype):
    t, d = x.shape
    tm = ROW_TILE
    return pl.pallas_call(
        _norm_body, grid=(t // tm,),
        in_specs=[_row_spec(tm, d), _full_spec((1, d))],
        out_specs=_row_spec(tm, d),
        out_shape=jax.ShapeDtypeStruct((t, d), out_dtype),
        compiler_params=_params("parallel"), name="rms_norm_rows",
    )(x, gain.reshape(1, -1))


def expert_choice_ffn(x, layer, p):
    t, dm = x.shape
    flat = rms_norm_rows(x, p['norm_ffn'][layer], F32)
    aff = jax.nn.softmax(jnp.dot(flat, p['router'][layer], precision=lax.Precision.HIGHEST), axis=-1)
    cap = EC_CAPACITY * t // N_EXPERTS
    gate, idx = lax.top_k(aff.T, cap)
    xe = flat.astype(BF16)[idx]
    ye = expert_ffn(xe, gate[..., None], p['w_gate'][layer], p['w_up'][layer], p['w_down'][layer])
    return x.at[idx.reshape(-1)].add(ye.reshape(-1, dm))


def trunk(x, mem, p):
    b, n, d = x.shape
    x = x.reshape(b * n, d)
    for layer in range(DEPTH):
        i = layer // 2
        if layer % 2 == 0:
            za, zb = in_proj_ab(x, p['norm_mix'][layer], p['w_in_ab'][i], p['gla_gw_f'][i],
                                p['gla_gb_f'][i], p['gla_gw_b'][i], p['gla_gb_b'][i])
            o_a = neighbourhood_attention(za, p['na_rpb'][i], b, n)
            o_fw, o_bw = gla_bidirectional(zb, b, n)
            x = out_proj_ab(o_a, o_fw, o_bw, zb, p['gla_norm'][i], p['w_out_ab'][i], x)
        else:
            bg, u = in_proj_c(x, p['norm_mix'][layer], p['w_in_c'][i])
            x = conv_out_proj(u, bg, p['conv_w'][i], p['w_out_c'][i], x, n)
        x = memory_attention(x, mem, p['norm_mem'][layer], p['w_mq'][layer], p['w_mk'][layer],
                             p['w_mv'][layer], p['w_mo'][layer], n)
        x = expert_choice_ffn(x, layer, p)
    return rms_norm_rows(x, p['final_norm'], F32).reshape(b, n, d)


def kernel(x_prompt, x_sample, mem_prompt, mem_sample, w_in_ab, na_rpb, gla_gw_f, gla_gb_f, gla_gw_b,
           gla_gb_b, gla_norm, w_out_ab, w_in_c, conv_w, w_out_c, norm_mix, norm_mem, norm_ffn,
           w_mq, w_mk, w_mv, w_mo, router, w_gate, w_up, w_down, final_norm):
    p = dict(w_in_ab=w_in_ab, na_rpb=na_rpb, gla_gw_f=gla_gw_f, gla_gb_f=gla_gb_f, gla_gw_b=gla_gw_b,
             gla_gb_b=gla_gb_b, gla_norm=gla_norm, w_out_ab=w_out_ab, w_in_c=w_in_c, conv_w=conv_w,
             w_out_c=w_out_c, norm_mix=norm_mix, norm_mem=norm_mem, norm_ffn=norm_ffn, w_mq=w_mq,
             w_mk=w_mk, w_mv=w_mv, w_mo=w_mo, router=router, w_gate=w_gate.astype(BF16),
             w_up=w_up.astype(BF16), w_down=w_down.astype(BF16), final_norm=final_norm)
    y_prompt = trunk(x_prompt, mem_prompt, p)
    y_sample = trunk(x_sample, mem_sample, p)
    return (y_prompt, y_sample)
```

```python
import functools

import jax
import jax.numpy as jnp
import numpy as np
from jax import lax
from jax.experimental import pallas as pl
from jax.experimental.pallas import tpu as pltpu

D_MODEL = 1024
DEPTH = 4
GRID_W = 64
EPS = 1e-6
NEG_INF = -1e30
NA_HEADS = 8
NA_HEAD_DIM = 64
NA_WIN_R = 8
NA_WIN_C = 16
NA_WIDTH = NA_HEADS * NA_HEAD_DIM
GLA_HEADS = 4
GLA_DK = 64
GLA_DV = 128
GLA_RANK = 16
GLA_TAU = 16.0
GLA_CHUNK = 64
GLA_KW = GLA_HEADS * GLA_DK
GLA_VW = GLA_HEADS * GLA_DV
CONV_W = 3
MEM_HEADS = 4
MEM_HEAD_DIM = D_MODEL // MEM_HEADS
N_EXPERTS = 16
D_EXPERT = 2 * D_MODEL
EC_CAPACITY = 2

VMEM_LIMIT_BYTES = 48 * 1024 * 1024
ROW_TILE = 512
SUBLANES = 8
NA_ROWS_PER_BLOCK = 4
NA_BLOCK_TOKENS = NA_ROWS_PER_BLOCK * GRID_W
GLA_BLOCK = 512
LANES = 128
NOT_SELECTED = -(1 << 24)
MOE_TILE = 256
MOE_SLAB = 64
FFN_TILE = 256
BF16 = jnp.bfloat16
F32 = jnp.float32


def _params(*sem):
    return pltpu.CompilerParams(dimension_semantics=sem, vmem_limit_bytes=VMEM_LIMIT_BYTES)


def _rms(x, g):
    return x * lax.rsqrt(jnp.mean(x * x, axis=-1, keepdims=True) + EPS) * g


def _row_spec(tm, n, col=0):
    return pl.BlockSpec((tm, n), lambda i: (i, col))


def _full_spec(shape):
    return pl.BlockSpec(shape, lambda *_: (0,) * len(shape))


def _mm_body(x_ref, w_ref, o_ref):
    o_ref[...] = jnp.dot(x_ref[...].astype(BF16), w_ref[...],
                         preferred_element_type=F32).astype(o_ref.dtype)


def matmul(x, w, out_dtype):
    m, k = x.shape
    n = w.shape[1]
    tm = min(ROW_TILE, m)
    return pl.pallas_call(
        _mm_body, grid=(m // tm,),
        in_specs=[_row_spec(tm, k), _full_spec((k, n))],
        out_specs=_row_spec(tm, n),
        out_shape=jax.ShapeDtypeStruct((m, n), out_dtype),
        compiler_params=_params("parallel"), name="row_matmul",
    )(x, w)


def _log_sigmoid(x):
    return jnp.minimum(x, 0.0) - jnp.log1p(jnp.exp(-jnp.abs(x)))


def _in_ab_body(x_ref, g_ref, wa_ref, wb_ref, wg_ref, gwf_ref, gbf_ref, gwb_ref, gbb_ref,
                za_ref, zb_ref):
    hn = _rms(x_ref[...], g_ref[...]).astype(BF16)
    za_ref[...] = jnp.dot(hn, wa_ref[...], preferred_element_type=F32).astype(za_ref.dtype)
    nb = wb_ref.shape[1]
    zb_ref[:, :nb] = jnp.dot(hn, wb_ref[...], preferred_element_type=F32)
    lowrank = jnp.dot(hn, wg_ref[...], preferred_element_type=F32)
    pre_f = jnp.dot(lowrank[:, :GLA_RANK], gwf_ref[...], preferred_element_type=F32,
                    precision=lax.Precision.HIGHEST) + gbf_ref[...]
    pre_b = jnp.dot(lowrank[:, GLA_RANK:], gwb_ref[...], preferred_element_type=F32,
                    precision=lax.Precision.HIGHEST) + gbb_ref[...]
    zb_ref[:, nb:nb + GLA_KW] = _log_sigmoid(pre_f) / GLA_TAU
    zb_ref[:, nb + GLA_KW:] = _log_sigmoid(pre_b) / GLA_TAU


def in_proj_ab(x, gain, w_in, gwf, gbf, gwb, gbb):
    t = x.shape[0]
    tm = ROW_TILE
    na_w = 3 * NA_WIDTH
    gl_w = 2 * GLA_KW + 2 * GLA_VW
    o = np.cumsum([0, NA_WIDTH, NA_WIDTH, NA_WIDTH, GLA_KW, GLA_KW, GLA_VW, GLA_RANK, GLA_RANK, GLA_VW])
    wa = w_in[:, :o[3]].astype(BF16)
    wb = jnp.concatenate([w_in[:, o[3]:o[6]], w_in[:, o[8]:o[9]]], axis=1).astype(BF16)
    wg = w_in[:, o[6]:o[8]].astype(BF16)
    return pl.pallas_call(
        _in_ab_body, grid=(t // tm,),
        in_specs=[_row_spec(tm, D_MODEL), _full_spec((1, D_MODEL)), _full_spec((D_MODEL, na_w)),
                  _full_spec((D_MODEL, gl_w)), _full_spec((D_MODEL, 2 * GLA_RANK)),
                  _full_spec((GLA_RANK, GLA_KW)), _full_spec((1, GLA_KW)),
                  _full_spec((GLA_RANK, GLA_KW)), _full_spec((1, GLA_KW))],
        out_specs=[_row_spec(tm, na_w), _row_spec(tm, gl_w + 2 * GLA_KW)],
        out_shape=[jax.ShapeDtypeStruct((t, na_w), BF16),
                   jax.ShapeDtypeStruct((t, gl_w + 2 * GLA_KW), F32)],
        compiler_params=_params("parallel"), name="in_proj_ab",
    )(x, gain.reshape(1, -1), wa, wb, wg, gwf, gbf.reshape(1, -1), gwb, gbb.reshape(1, -1))


def na_bias_table(rpb):
    rb = NA_ROWS_PER_BLOCK
    a = np.arange(rb)[:, None, None, None]
    c = np.arange(GRID_W)[None, :, None, None]
    u = np.arange(3 * rb)[None, None, :, None]
    kc = np.arange(GRID_W)[None, None, None, :]
    wstart = np.clip(c - NA_WIN_C // 2, 0, GRID_W - NA_WIN_C)
    col_ok = (kc >= wstart) & (kc < wstart + NA_WIN_C)
    dc = np.clip(kc - c + NA_WIN_C - 1, 0, 2 * NA_WIN_C - 2)
    dr = np.clip(u - a + NA_WIN_R // 2 - 1, 0, 2 * NA_WIN_R - 2)
    row_ok = [
        (u >= rb) & (u < rb + NA_WIN_R) & (a >= 0),
        (u - a >= 0) & (u - a < NA_WIN_R),
        (u >= 0) & (u < NA_WIN_R) & (a >= 0),
    ]
    shape = (rb, GRID_W, 3 * rb, GRID_W)
    flat = (NA_BLOCK_TOKENS, 3 * NA_BLOCK_TOKENS)
    dr_b = np.broadcast_to(dr, shape).reshape(flat)
    dc_b = np.broadcast_to(dc, shape).reshape(flat)
    vals = rpb.astype(F32)[:, dr_b, dc_b]
    out = []
    for ok in row_ok:
        m = np.broadcast_to(ok & col_ok, shape).reshape(flat)
        out.append(jnp.where(jnp.asarray(m)[None], vals, NEG_INF))
    return jnp.stack(out)


def _na_body(q_ref, kp_ref, kc_ref, kn_ref, vp_ref, vc_ref, vn_ref, bias_ref, o_ref):
    lane = lax.broadcasted_iota(jnp.int32, (1, 2 * NA_HEAD_DIM), 1)
    scale = NA_HEAD_DIM ** -0.5
    nt = (((1,), (1,)), ((), ()))
    for hp in range(NA_HEADS // 2):
        cols = slice(hp * 2 * NA_HEAD_DIM, (hp + 1) * 2 * NA_HEAD_DIM)
        q = q_ref[:, cols] * scale
        ks = [r[:, cols] for r in (kp_ref, kc_ref, kn_ref)]
        vs = [r[:, cols] for r in (vp_ref, vc_ref, vn_ref)]
        o_pair = None
        for sub in range(2):
            sel = (lane < NA_HEAD_DIM) if sub == 0 else (lane >= NA_HEAD_DIM)
            qm = jnp.where(sel, q, jnp.zeros_like(q))
            s = jnp.concatenate(
                [lax.dot_general(qm, kk, nt, preferred_element_type=F32) for kk in ks], axis=1)
            s = s + bias_ref[0, 2 * hp + sub]
            p = jnp.exp(s - jnp.max(s, axis=-1, keepdims=True))
            l = jnp.sum(p, axis=-1, keepdims=True)
            pb = p.astype(BF16)
            o = None
            for j, vv in enumerate(vs):
                t = jnp.dot(pb[:, j * NA_BLOCK_TOKENS:(j + 1) * NA_BLOCK_TOKENS], vv,
                            preferred_element_type=F32)
                o = t if o is None else o + t
            o = o / l
            o_pair = o if o_pair is None else jnp.where(sel, o, o_pair)
        o_ref[:, cols] = o_pair.astype(o_ref.dtype)


def neighbourhood_attention(za, rpb, batch, n):
    t = za.shape[0]
    bt = NA_BLOCK_TOKENS
    nblk = n // bt
    assert n % bt == 0 and nblk >= 3
    bias = na_bias_table(rpb)

    def qmap(b, i):
        return (b * nblk + i, 0)

    def kmap(d, col):
        return lambda b, i: (b * nblk + jnp.clip(i + d, 0, nblk - 1), col)

    def bmap(b, i):
        return (jnp.where(i == 0, 0, jnp.where(i == nblk - 1, 2, 1)), 0, 0, 0)

    def blk(m):
        return pl.BlockSpec((bt, NA_WIDTH), m)

    return pl.pallas_call(
        _na_body, grid=(batch, nblk),
        in_specs=[blk(qmap), blk(kmap(-1, 1)), blk(kmap(0, 1)), blk(kmap(1, 1)),
                  blk(kmap(-1, 2)), blk(kmap(0, 2)), blk(kmap(1, 2)),
                  pl.BlockSpec((1, NA_HEADS, bt, 3 * bt), bmap)],
        out_specs=blk(qmap),
        out_shape=jax.ShapeDtypeStruct((t, NA_WIDTH), BF16),
        compiler_params=_params("parallel", "parallel"), name="neighbourhood_attention",
    )(za, za, za, za, za, za, za, bias)


def _split_dot(a_bf, x, dims):
    hi = x.astype(BF16)
    lo = (x - hi.astype(F32)).astype(BF16)
    return (lax.dot_general(a_bf, hi, dims, preferred_element_type=F32)
            + lax.dot_general(a_bf, lo, dims, preferred_element_type=F32))


def _split_dot_t(x, a_bf, dims):
    hi = x.astype(BF16)
    lo = (x - hi.astype(F32)).astype(BF16)
    return (lax.dot_general(hi, a_bf, dims, preferred_element_type=F32)
            + lax.dot_general(lo, a_bf, dims, preferred_element_type=F32))


def _gla_body(qf_ref, kf_ref, vf_ref, gf_ref, qb_ref, kb_ref, vb_ref, gb_ref,
              of_ref, ob_ref, sf_ref, sb_ref):
    c = GLA_CHUNK
    nchunk = GLA_BLOCK // c

    @pl.when(pl.program_id(1) == 0)
    def _():
        sf_ref[...] = jnp.zeros_like(sf_ref)
        sb_ref[...] = jnp.zeros_like(sb_ref)

    row = lax.broadcasted_iota(jnp.int32, (c, c), 0)
    col = lax.broadcasted_iota(jnp.int32, (c, c), 1)
    tril = row >= col
    triu = row <= col
    ones_cc = jnp.ones((c, c), BF16)
    ones_cv = jnp.ones((c, GLA_DV), BF16)
    mm = (((1,), (0,)), ((), ()))
    nt = (((1,), (1,)), ((), ()))
    tn = (((0,), (0,)), ((), ()))

    def one_chunk(j, q_ref, k_ref, v_ref, g_ref, o_ref, s_ref, keep):
        rows = pl.ds(pl.multiple_of(j * c, c), c)
        g = g_ref[rows, :]
        cum = _split_dot(keep.astype(BF16), g, mm)
        tot_rows = _split_dot(ones_cc, g, mm)
        tot_cols = _split_dot_t(g, ones_cv, tn)
        q = q_ref[rows, :] * (GLA_DK ** -0.5)
        k = k_ref[rows, :]
        q_t = (q * jnp.exp(cum)).astype(BF16)
        k_t = (k * jnp.exp(-cum)).astype(BF16)
        k_d = (k * jnp.exp(tot_rows - cum)).astype(BF16)
        v = v_ref[rows, :].astype(BF16)
        for h in range(GLA_HEADS):
            kc = slice(h * GLA_DK, (h + 1) * GLA_DK)
            vc = slice(h * GLA_DV, (h + 1) * GLA_DV)
            a = lax.dot_general(q_t[:, kc], k_t[:, kc], nt, preferred_element_type=F32)
            a = jnp.where(keep, a, 0.0).astype(BF16)
            s_prev = s_ref[h]
            o_ref[rows, vc] = (jnp.dot(a, v[:, vc], preferred_element_type=F32)
                               + jnp.dot(q_t[:, kc], s_prev.astype(BF16), preferred_element_type=F32))
            ds = lax.dot_general(k_d[:, kc], v[:, vc], tn, preferred_element_type=F32)
            s_ref[h] = s_prev * jnp.exp(tot_cols[kc, :]) + ds

    def step(j, carry):
        one_chunk(j, qf_ref, kf_ref, vf_ref, gf_ref, of_ref, sf_ref, tril)
        one_chunk(nchunk - 1 - j, qb_ref, kb_ref, vb_ref, gb_ref, ob_ref, sb_ref, triu)
        return carry

    lax.fori_loop(0, nchunk, step, 0)


def gla_bidirectional(zb, batch, n):
    t = zb.shape[0]
    nb = n // GLA_BLOCK
    assert n % GLA_BLOCK == 0

    def fwd(col):
        return lambda b, i: (b * nb + i, col)

    def bwd(col):
        return lambda b, i: (b * nb + nb - 1 - i, col)

    def kw(m):
        return pl.BlockSpec((GLA_BLOCK, GLA_KW), m)

    def vw(m):
        return pl.BlockSpec((GLA_BLOCK, GLA_VW), m)

    return pl.pallas_call(
        _gla_body, grid=(batch, nb),
        in_specs=[kw(fwd(0)), kw(fwd(1)), vw(fwd(1)), kw(fwd(6)),
                  kw(bwd(0)), kw(bwd(1)), vw(bwd(1)), kw(bwd(7))],
        out_specs=[vw(fwd(0)), vw(bwd(0))],
        out_shape=[jax.ShapeDtypeStruct((t, GLA_VW), F32)] * 2,
        scratch_shapes=[pltpu.VMEM((GLA_HEADS, GLA_DK, GLA_DV), F32)] * 2,
        compiler_params=_params("parallel", "arbitrary"), name="gla_bidirectional",
    )(zb, zb, zb, zb, zb, zb, zb, zb)


def _silu(x):
    return x * jax.nn.sigmoid(x)


def _out_ab_body(oa_ref, of_ref, ob_ref, r_ref, gn_ref, wa_ref, wb_ref, x_ref, o_ref):
    o = of_ref[...] + ob_ref[...]
    r = r_ref[...]
    parts = []
    for h in range(GLA_HEADS):
        vc = slice(h * GLA_DV, (h + 1) * GLA_DV)
        parts.append((_rms(o[:, vc], gn_ref[...]) * _silu(r[:, vc])).astype(BF16))
    o_b = jnp.concatenate(parts, axis=1)
    o_ref[...] = (jnp.dot(oa_ref[...], wa_ref[...], preferred_element_type=F32)
                  + jnp.dot(o_b, wb_ref[...], preferred_element_type=F32) + x_ref[...])


def out_proj_ab(o_a, o_fw, o_bw, zb, gla_norm, w_out, x):
    t = x.shape[0]
    tm = ROW_TILE
    w = w_out.astype(BF16)
    return pl.pallas_call(
        _out_ab_body, grid=(t // tm,),
        in_specs=[_row_spec(tm, NA_WIDTH), _row_spec(tm, GLA_VW), _row_spec(tm, GLA_VW),
                  _row_spec(tm, GLA_VW, col=2), _full_spec((1, GLA_DV)),
                  _full_spec((NA_WIDTH, D_MODEL)), _full_spec((GLA_VW, D_MODEL)),
                  _row_spec(tm, D_MODEL)],
        out_specs=_row_spec(tm, D_MODEL),
        out_shape=jax.ShapeDtypeStruct((t, D_MODEL), F32),
        compiler_params=_params("parallel"), name="out_proj_ab",
    )(o_a, o_fw, o_bw, zb, gla_norm.reshape(1, -1), w[:NA_WIDTH], w[NA_WIDTH:], x)


def _in_c_body(x_ref, g_ref, wbg_ref, wcg_ref, wxt_ref, bg_ref, u_ref):
    hn = _rms(x_ref[...], g_ref[...]).astype(BF16)
    bg_ref[...] = jnp.dot(hn, wbg_ref[...], preferred_element_type=F32)
    u_ref[...] = (jnp.dot(hn, wcg_ref[...], preferred_element_type=F32)
                  * jnp.dot(hn, wxt_ref[...], preferred_element_type=F32))


def in_proj_c(x, gain, w_in):
    t = x.shape[0]
    tm = ROW_TILE
    w = w_in.astype(BF16)
    d = D_MODEL
    return pl.pallas_call(
        _in_c_body, grid=(t // tm,),
        in_specs=[_row_spec(tm, d), _full_spec((1, d))] + [_full_spec((d, d))] * 3,
        out_specs=[_row_spec(tm, d)] * 2,
        out_shape=[jax.ShapeDtypeStruct((t, d), F32)] * 2,
        compiler_params=_params("parallel"), name="in_proj_c",
    )(x, gain.reshape(1, -1), w[:, :d], w[:, d:2 * d], w[:, 2 * d:])


def _conv_out_body(u_ref, up_ref, un_ref, bg_ref, cw_ref, w_ref, x_ref, o_ref, *, tiles_per_seq):
    i = pl.program_id(0)
    tm = u_ref.shape[0]
    u = u_ref[...]
    row = lax.broadcasted_iota(jnp.int32, (tm, 1), 0)
    first = (i % tiles_per_seq) == 0
    last = (i % tiles_per_seq) == tiles_per_seq - 1
    prev_row = jnp.where(first, 0.0, up_ref[SUBLANES - 1:SUBLANES, :])
    next_row = jnp.where(last, 0.0, un_ref[0:1, :])
    u_prev = jnp.where(row == 0, prev_row, pltpu.roll(u, 1, axis=0))
    u_next = jnp.where(row == tm - 1, next_row, pltpu.roll(u, tm - 1, axis=0))
    conv = cw_ref[0:1, :] * u_prev + cw_ref[1:2, :] * u + cw_ref[2:3, :] * u_next
    o_ref[...] = (jnp.dot((bg_ref[...] * conv).astype(BF16), w_ref[...], preferred_element_type=F32)
                  + x_ref[...])


def conv_out_proj(u, bg, conv_w, w_out, x, n):
    t = x.shape[0]
    tm = ROW_TILE
    d = D_MODEL
    per = tm // SUBLANES
    nhalo = t // SUBLANES
    assert n % tm == 0
    return pl.pallas_call(
        functools.partial(_conv_out_body, tiles_per_seq=n // tm), grid=(t // tm,),
        in_specs=[_row_spec(tm, d),
                  pl.BlockSpec((SUBLANES, d), lambda i: (jnp.maximum(i * per - 1, 0), 0)),
                  pl.BlockSpec((SUBLANES, d), lambda i: (jnp.minimum((i + 1) * per, nhalo - 1), 0)),
                  _row_spec(tm, d), _full_spec((CONV_W, d)), _full_spec((d, d)), _row_spec(tm, d)],
        out_specs=_row_spec(tm, d),
        out_shape=jax.ShapeDtypeStruct((t, d), F32),
        compiler_params=_params("parallel"), name="conv_out_proj",
    )(u, u, u, bg, conv_w, w_out.astype(BF16), x)


def _mem_body(x_ref, g_ref, wq_ref, k_ref, v_ref, wo_ref, o_ref):
    x = x_ref[...]
    hn = _rms(x, g_ref[...]).astype(BF16)
    q = (jnp.dot(hn, wq_ref[...], preferred_element_type=F32) * MEM_HEAD_DIM ** -0.5).astype(BF16)
    nt = (((1,), (1,)), ((), ()))
    parts = []
    for h in range(MEM_HEADS):
        hc = slice(h * MEM_HEAD_DIM, (h + 1) * MEM_HEAD_DIM)
        s = lax.dot_general(q[:, hc], k_ref[0, :, hc], nt, preferred_element_type=F32)
        p = jnp.exp(s - jnp.max(s, axis=-1, keepdims=True))
        l = jnp.sum(p, axis=-1, keepdims=True)
        o = jnp.dot(p.astype(BF16), v_ref[0, :, hc], preferred_element_type=F32) / l
        parts.append(o.astype(BF16))
    o_ref[...] = jnp.dot(jnp.concatenate(parts, axis=1), wo_ref[...], preferred_element_type=F32) + x


def memory_attention(x, mem, gain, wq, wk, wv, wo, n):
    t, d = x.shape
    b, nm, _ = mem.shape
    tm = ROW_TILE
    per = n // tm
    memf = mem.reshape(b * nm, d)
    k = matmul(memf, wk.astype(BF16), BF16).reshape(b, nm, d)
    v = matmul(memf, wv.astype(BF16), BF16).reshape(b, nm, d)
    kv_spec = pl.BlockSpec((1, nm, d), lambda i: (i // per, 0, 0))
    return pl.pallas_call(
        _mem_body, grid=(t // tm,),
        in_specs=[_row_spec(tm, d), _full_spec((1, d)), _full_spec((d, d)), kv_spec, kv_spec,
                  _full_spec((d, d))],
        out_specs=_row_spec(tm, d),
        out_shape=jax.ShapeDtypeStruct((t, d), F32),
        compiler_params=_params("parallel"), name="memory_attention",
    )(x, gain.reshape(1, -1), wq.astype(BF16), k, v, wo.astype(BF16))


def _router_body(x_ref, g_ref, r_ref, aff_ref):
    hn = _rms(x_ref[...], g_ref[...])
    logits = lax.dot_general(r_ref[...], hn, (((1,), (1,)), ((), ())), preferred_element_type=F32,
                             precision=lax.Precision.HIGHEST)
    p = jnp.exp(logits - jnp.max(logits, axis=0, keepdims=True))
    aff_ref[...] = p / jnp.sum(p, axis=0, keepdims=True)


def router_affinity(x, gain, router):
    t, d = x.shape
    tm = ROW_TILE
    return pl.pallas_call(
        _router_body, grid=(t // tm,),
        in_specs=[_row_spec(tm, d), _full_spec((1, d)), _full_spec((N_EXPERTS, d))],
        out_specs=pl.BlockSpec((N_EXPERTS, tm), lambda i: (0, i)),
        out_shape=jax.ShapeDtypeStruct((N_EXPERTS, t), F32),
        compiler_params=_params("parallel"), name="router_affinity",
    )(x, gain.reshape(1, -1), router.T)


def _select_body(aff_ref, slot_ref, base_ref, cnt_ref, *, cap):
    e, nc, _ = aff_ref.shape
    per = MOE_TILE // LANES
    bits = pltpu.bitcast(aff_ref[...], jnp.int32)

    def count(mask):
        s = jnp.sum(mask.astype(jnp.int32), axis=1, keepdims=True)
        return jnp.sum(s, axis=2, keepdims=True)

    def bisect(b, thr):
        cand = thr | (jnp.int32(1) << (30 - b))
        return jnp.where(count(bits >= cand) >= cap, cand, thr)

    thr = lax.fori_loop(0, 31, bisect, jnp.zeros((e, 1, 1), jnp.int32))
    gt = bits > thr
    eq = bits == thr
    need_eq = cap - count(gt)

    li = lax.broadcasted_iota(jnp.int32, (LANES, LANES), 0)
    lj = lax.broadcasted_iota(jnp.int32, (LANES, LANES), 1)
    incl = (li <= lj).astype(BF16)
    ci = lax.broadcasted_iota(jnp.int32, (nc, nc), 0)
    cj = lax.broadcasted_iota(jnp.int32, (nc, nc), 1)
    strict = (cj < ci).astype(BF16)
    same_tile = (cj // per == ci // per).astype(BF16)
    same_tile_before = ((cj // per == ci // per) & (cj < ci)).astype(BF16)
    tiles_before = ((cj // per < ci // per) & (cj % per == 0)).astype(BF16)

    def chunk_prefix(m):
        within = jnp.dot(m, incl, preferred_element_type=F32)
        tot = jnp.broadcast_to(within[:, LANES - 1:LANES], (nc, LANES)).astype(BF16)
        return within, tot

    for x in range(e):
        eq_x = eq[x].astype(BF16)
        within, tot = chunk_prefix(eq_x)
        rank_eq = within - eq_x.astype(F32) + jnp.dot(strict, tot, preferred_element_type=F32)
        sel = gt[x] | (eq[x] & (rank_eq < need_eq[x].astype(F32)))
        sel_b = sel.astype(BF16)
        within, tot = chunk_prefix(sel_b)
        tile_cnt = jnp.dot(same_tile, tot, preferred_element_type=F32).astype(jnp.int32)
        padded = ((tile_cnt + (SUBLANES - 1)) // SUBLANES) * SUBLANES
        base = jnp.dot(tiles_before, padded.astype(BF16), preferred_element_type=F32)
        in_tile = jnp.dot(same_tile_before, tot, preferred_element_type=F32)
        pos = (base + in_tile + within - sel_b.astype(F32)).astype(jnp.int32)
        slot_ref[x] = jnp.where(sel, pos, NOT_SELECTED)
        base_ref[x] = base.astype(jnp.int32)
        cnt_ref[x] = tile_cnt


def select_slots(aff_t, cap):
    e, t = aff_t.shape
    nc = t // LANES
    per = MOE_TILE // LANES
    slot, base, cnt = pl.pallas_call(
        functools.partial(_select_body, cap=cap),
        out_shape=[jax.ShapeDtypeStruct((e, nc, LANES), jnp.int32)] * 3,
        compiler_params=pltpu.CompilerParams(vmem_limit_bytes=VMEM_LIMIT_BYTES), name="select_slots",
    )(aff_t.reshape(e, nc, LANES))
    base = base[:, ::per, 0]
    cnt = cnt[:, ::per, 0]
    rows = base[:, -1] + ((cnt[:, -1] + SUBLANES - 1) // SUBLANES) * SUBLANES
    return slot.reshape(e, t), base.reshape(-1), cnt.reshape(-1), rows


def slab_rows(t, cap):
    worst = cap + (SUBLANES - 1) * (t // MOE_TILE)
    return -(-worst // FFN_TILE) * FFN_TILE + MOE_SLAB


def _one_hot_t(slot_ref, lo, start, width):
    tm = slot_ref.shape[1]
    w = lax.broadcasted_iota(jnp.int32, (width, tm), 0)
    rows = []
    for e in range(N_EXPERTS):
        s = slot_ref[e:e + 1, :]
        rows.append((s - start[e] == w) & (s >= lo[e]) & (s < lo[e] + width))
    return jnp.concatenate(rows, axis=0)


def _tile_scalars(base_ref, cnt_ref, i, nt):
    lo = [base_ref[e * nt + i] for e in range(N_EXPERTS)]
    cnt = [cnt_ref[e * nt + i] for e in range(N_EXPERTS)]
    return lo, cnt


def _rounds(cnt):
    m = cnt[0]
    for c in cnt[1:]:
        m = jnp.maximum(m, c)
    return (m + MOE_SLAB - 1) // MOE_SLAB


def _dispatch_body(base_ref, cnt_ref, x_ref, g_ref, slot_ref, aff_ref, xe_in, ge_in, xe_ref, ge_ref,
                   xbuf, gbuf, xover, gover, sem, osem, *, nt):
    del xe_in, ge_in
    i = pl.program_id(0)
    cur = i % 2
    w_ = MOE_SLAB
    hn = _rms(x_ref[...], g_ref[...]).astype(BF16)
    lo, cnt = _tile_scalars(base_ref, cnt_ref, i, nt)

    def stage(xdst, gdst, lo_k):
        hot = _one_hot_t(slot_ref, lo_k, lo_k, w_)
        rows = jnp.dot(hot.astype(BF16), hn, preferred_element_type=F32)
        hot_f = hot.astype(F32)
        for e in range(N_EXPERTS):
            sl = slice(e * w_, (e + 1) * w_)
            xdst[e] = rows[sl]
            gate = jnp.sum(hot_f[sl] * aff_ref[e:e + 1, :], axis=1, keepdims=True)
            gdst[e] = jnp.broadcast_to(gate, (w_, LANES))

    def copies(xsrc, gsrc, lo_k, e, xs, gs):
        dst = pl.ds(pl.multiple_of(lo_k[e], SUBLANES), w_)
        return (pltpu.make_async_copy(xsrc.at[e], xe_ref.at[e, dst], xs),
                pltpu.make_async_copy(gsrc.at[e], ge_ref.at[e, dst], gs))

    stage(xbuf.at[cur], gbuf.at[cur], lo)

    @pl.when(i > 0)
    def _():
        lo_p, _ = _tile_scalars(base_ref, cnt_ref, i - 1, nt)
        for e in range(N_EXPERTS):
            for cp in copies(xbuf.at[1 - cur], gbuf.at[1 - cur], lo_p, e, sem.at[1 - cur, 0], sem.at[1 - cur, 1]):
                cp.wait()

    for e in range(N_EXPERTS):
        for cp in copies(xbuf.at[cur], gbuf.at[cur], lo, e, sem.at[cur, 0], sem.at[cur, 1]):
            cp.start()

    def extra_round(k, carry):
        lo_k = [l + k * w_ for l in lo]
        stage(xover, gover, lo_k)
        for e in range(N_EXPERTS):
            @pl.when(cnt[e] > k * w_)
            def _():
                for cp in copies(xover, gover, lo_k, e, osem.at[0], osem.at[1]):
                    cp.start()
        for e in range(N_EXPERTS):
            @pl.when(cnt[e] > k * w_)
            def _():
                for cp in copies(xover, gover, lo_k, e, osem.at[0], osem.at[1]):
                    cp.wait()
        return carry

    lax.fori_loop(1, _rounds(cnt), extra_round, 0)

    @pl.when(i == nt - 1)
    def _():
        for e in range(N_EXPERTS):
            for cp in copies(xbuf.at[cur], gbuf.at[cur], lo, e, sem.at[cur, 0], sem.at[cur, 1]):
                cp.wait()


def dispatch(x, gain, slot, aff_t, base, cnt, rows_alloc):
    t, d = x.shape
    tm = MOE_TILE
    nt = t // tm
    any_spec = pl.BlockSpec(memory_space=pl.ANY)
    grid_spec = pltpu.PrefetchScalarGridSpec(
        num_scalar_prefetch=2, grid=(nt,),
        in_specs=[pl.BlockSpec((tm, d), lambda i, *_: (i, 0)), pl.BlockSpec((1, d), lambda i, *_: (0, 0)),
                  pl.BlockSpec((N_EXPERTS, tm), lambda i, *_: (0, i)),
                  pl.BlockSpec((N_EXPERTS, tm), lambda i, *_: (0, i)), any_spec, any_spec],
        out_specs=[any_spec, any_spec],
        scratch_shapes=[pltpu.VMEM((2, N_EXPERTS, MOE_SLAB, d), F32),
                        pltpu.VMEM((2, N_EXPERTS, MOE_SLAB, LANES), F32),
                        pltpu.VMEM((N_EXPERTS, MOE_SLAB, d), F32),
                        pltpu.VMEM((N_EXPERTS, MOE_SLAB, LANES), F32),
                        pltpu.SemaphoreType.DMA((2, 2)), pltpu.SemaphoreType.DMA((2,))])
    return pl.pallas_call(
        functools.partial(_dispatch_body, nt=nt), grid_spec=grid_spec,
        out_shape=[jax.ShapeDtypeStruct((N_EXPERTS, rows_alloc, d), F32),
                   jax.ShapeDtypeStruct((N_EXPERTS, rows_alloc, LANES), F32)],
        input_output_aliases={6: 0, 7: 1},
        compiler_params=_params("arbitrary"), name="moe_dispatch",
    )(base, cnt, x, gain.reshape(1, -1), slot, aff_t,
      jnp.zeros((N_EXPERTS, rows_alloc, d), F32), jnp.zeros((N_EXPERTS, rows_alloc, LANES), F32))


def _ffn_body(rows_ref, xe_ref, gate_ref, wg_ref, wu_ref, wd_ref, o_ref):
    used = pl.program_id(1) * FFN_TILE < rows_ref[pl.program_id(0)]

    @pl.when(jnp.logical_not(used))
    def _():
        o_ref[...] = jnp.zeros_like(o_ref)

    @pl.when(used)
    def _():
        xe = xe_ref[0].astype(BF16)
        a = jnp.dot(xe, wg_ref[0], preferred_element_type=F32)
        b = jnp.dot(xe, wu_ref[0], preferred_element_type=F32)
        hid = (_silu(a) * b).astype(BF16)
        o_ref[0] = jnp.dot(hid, wd_ref[0], preferred_element_type=F32) * gate_ref[0][:, :1]


def expert_ffn(xe, gate, wg, wu, wd, rows):
    e, rows_alloc, d = xe.shape
    f = wg.shape[-1]
    tm = FFN_TILE
    grid_spec = pltpu.PrefetchScalarGridSpec(
        num_scalar_prefetch=1, grid=(e, (rows_alloc - MOE_SLAB) // tm),
        in_specs=[
            pl.BlockSpec((1, tm, d), lambda i, j, *_: (i, j, 0)),
            pl.BlockSpec((1, tm, LANES), lambda i, j, *_: (i, j, 0)),
            pl.BlockSpec((1, d, f), lambda i, j, *_: (i, 0, 0)),
            pl.BlockSpec((1, d, f), lambda i, j, *_: (i, 0, 0)),
            pl.BlockSpec((1, f, d), lambda i, j, *_: (i, 0, 0)),
        ],
        out_specs=pl.BlockSpec((1, tm, d), lambda i, j, *_: (i, j, 0)))
    return pl.pallas_call(
        _ffn_body, grid_spec=grid_spec,
        out_shape=jax.ShapeDtypeStruct((e, rows_alloc - MOE_SLAB, d), F32),
        compiler_params=_params("parallel", "parallel"), name="expert_ffn",
    )(rows, xe, gate, wg, wu, wd)


def _combine_body(base_ref, cnt_ref, rows_ref, x_ref, slot_ref, y_ref, o_ref, ybuf, yover, sem, osem, *, nt):
    i = pl.program_id(0)
    cur = i % 2
    w_ = MOE_SLAB
    tn = (((0,), (0,)), ((), ()))

    def window(lo_k):
        return [pl.multiple_of(jnp.minimum(lo_k[e], rows_ref[e] - w_), SUBLANES) for e in range(N_EXPERTS)]

    def fetch(step, buf, xs):
        lo_s, _ = _tile_scalars(base_ref, cnt_ref, step, nt)
        st = window(lo_s)
        return [pltpu.make_async_copy(y_ref.at[e, pl.ds(st[e], w_)], buf.at[e], xs)
                for e in range(N_EXPERTS)]

    @pl.when(i == 0)
    def _():
        yover[...] = jnp.zeros_like(yover)
        for cp in fetch(0, ybuf.at[0], sem.at[0]):
            cp.start()

    @pl.when(i + 1 < nt)
    def _():
        for cp in fetch(i + 1, ybuf.at[1 - cur], sem.at[1 - cur]):
            cp.start()

    for cp in fetch(i, ybuf.at[cur], sem.at[cur]):
        cp.wait()

    lo, cnt = _tile_scalars(base_ref, cnt_ref, i, nt)

    def scatter(buf, lo_k):
        hot = _one_hot_t(slot_ref, lo_k, window(lo_k), w_).astype(BF16)
        ys = buf[...].reshape(N_EXPERTS * w_, buf.shape[-1]).astype(BF16)
        return lax.dot_general(hot, ys, tn, preferred_element_type=F32)

    o_ref[...] = x_ref[...] + scatter(ybuf.at[cur], lo)

    def extra_round(k, carry):
        lo_k = [l + k * w_ for l in lo]
        st = window(lo_k)
        for e in range(N_EXPERTS):
            @pl.when(cnt[e] > k * w_)
            def _():
                pltpu.make_async_copy(y_ref.at[e, pl.ds(st[e], w_)], yover.at[e], osem).start()
        for e in range(N_EXPERTS):
            @pl.when(cnt[e] > k * w_)
            def _():
                pltpu.make_async_copy(y_ref.at[e, pl.ds(st[e], w_)], yover.at[e], osem).wait()
        o_ref[...] += scatter(yover, lo_k)
        return carry

    lax.fori_loop(1, _rounds(cnt), extra_round, 0)


def combine(x, slot, y, base, cnt, rows):
    t, d = x.shape
    tm = MOE_TILE
    nt = t // tm
    grid_spec = pltpu.PrefetchScalarGridSpec(
        num_scalar_prefetch=3, grid=(nt,),
        in_specs=[pl.BlockSpec((tm, d), lambda i, *_: (i, 0)),
                  pl.BlockSpec((N_EXPERTS, tm), lambda i, *_: (0, i)),
                  pl.BlockSpec(memory_space=pl.ANY)],
        out_specs=pl.BlockSpec((tm, d), lambda i, *_: (i, 0)),
        scratch_shapes=[pltpu.VMEM((2, N_EXPERTS, MOE_SLAB, d), F32),
                        pltpu.VMEM((N_EXPERTS, MOE_SLAB, d), F32),
                        pltpu.SemaphoreType.DMA((2,)), pltpu.SemaphoreType.DMA(())])
    return pl.pallas_call(
        functools.partial(_combine_body, nt=nt), grid_spec=grid_spec,
        out_shape=jax.ShapeDtypeStruct((t, d), F32),
        compiler_params=_params("arbitrary"), name="moe_combine",
    )(base, cnt, rows, x, slot, y)


def _norm_body(x_ref, g_ref, o_ref):
    o_ref[...] = _rms(x_ref[...], g_ref[...]).astype(o_ref.dtype)


def rms_norm_rows(x, gain, out_dtype):
    t, d = x.shape
    tm = ROW_TILE
    return pl.pallas_call(
        _norm_body, grid=(t // tm,),
        in_specs=[_row_spec(tm, d), _full_spec((1, d))],
        out_specs=_row_spec(tm, d),
        out_shape=jax.ShapeDtypeStruct((t, d), out_dtype),
        compiler_params=_params("parallel"), name="rms_norm_rows",
    )(x, gain.reshape(1, -1))


def expert_choice_ffn(x, layer, p):
    t, _ = x.shape
    gain = p['norm_ffn'][layer]
    cap = EC_CAPACITY * t // N_EXPERTS
    aff_t = router_affinity(x, gain, p['router'][layer])
    slot, base, cnt, rows = select_slots(aff_t, cap)
    xe, ge = dispatch(x, gain, slot, aff_t, base, cnt, slab_rows(t, cap))
    y = expert_ffn(xe, ge, p['w_gate'][layer], p['w_up'][layer], p['w_down'][layer], rows)
    return combine(x, slot, y, base, cnt, rows)


def trunk(x, mem, p):
    b, n, d = x.shape
    x = x.reshape(b * n, d)
    for layer in range(DEPTH):
        i = layer // 2
        if layer % 2 == 0:
            za, zb = in_proj_ab(x, p['norm_mix'][layer], p['w_in_ab'][i], p['gla_gw_f'][i],
                                p['gla_gb_f'][i], p['gla_gw_b'][i], p['gla_gb_b'][i])
            o_a = neighbourhood_attention(za, p['na_rpb'][i], b, n)
            o_fw, o_bw = gla_bidirectional(zb, b, n)
            x = out_proj_ab(o_a, o_fw, o_bw, zb, p['gla_norm'][i], p['w_out_ab'][i], x)
        else:
            bg, u = in_proj_c(x, p['norm_mix'][layer], p['w_in_c'][i])
            x = conv_out_proj(u, bg, p['conv_w'][i], p['w_out_c'][i], x, n)
        x = memory_attention(x, mem, p['norm_mem'][layer], p['w_mq'][layer], p['w_mk'][layer],
                             p['w_mv'][layer], p['w_mo'][layer], n)
        x = expert_choice_ffn(x, layer, p)
    return rms_norm_rows(x, p['final_norm'], F32).reshape(b, n, d)


def kernel(x_prompt, x_sample, mem_prompt, mem_sample, w_in_ab, na_rpb, gla_gw_f, gla_gb_f, gla_gw_b,
           gla_gb_b, gla_norm, w_out_ab, w_in_c, conv_w, w_out_c, norm_mix, norm_mem, norm_ffn,
           w_mq, w_mk, w_mv, w_mo, router, w_gate, w_up, w_down, final_norm):
    p = dict(w_in_ab=w_in_ab, na_rpb=na_rpb, gla_gw_f=gla_gw_f, gla_gb_f=gla_gb_f, gla_gw_b=gla_gw_b,
             gla_gb_b=gla_gb_b, gla_norm=gla_norm, w_out_ab=w_out_ab, w_in_c=w_in_c, conv_w=conv_w,
             w_out_c=w_out_c, norm_mix=norm_mix, norm_mem=norm_mem, norm_ffn=norm_ffn, w_mq=w_mq,
             w_mk=w_mk, w_mv=w_mv, w_mo=w_mo, router=router, w_gate=w_gate.astype(BF16),
             w_up=w_up.astype(BF16), w_down=w_down.astype(BF16), final_norm=final_norm)
    y_prompt = trunk(x_prompt, mem_prompt, p)
    y_sample = trunk(x_sample, mem_sample, p)
    return (y_prompt, y_sample)
```

```python
import functools

import jax
import jax.numpy as jnp
import numpy as np
from jax import lax
from jax.experimental import pallas as pl
from jax.experimental.pallas import tpu as pltpu

D_MODEL = 1024
DEPTH = 4
GRID_W = 64
EPS = 1e-6
NEG_INF = -1e30
NA_HEADS = 8
NA_HEAD_DIM = 64
NA_WIN_R = 8
NA_WIN_C = 16
NA_WIDTH = NA_HEADS * NA_HEAD_DIM
GLA_HEADS = 4
GLA_DK = 64
GLA_DV = 128
GLA_RANK = 16
GLA_TAU = 16.0
GLA_CHUNK = 64
GLA_KW = GLA_HEADS * GLA_DK
GLA_VW = GLA_HEADS * GLA_DV
CONV_W = 3
MEM_HEADS = 4
MEM_HEAD_DIM = D_MODEL // MEM_HEADS
N_EXPERTS = 16
D_EXPERT = 2 * D_MODEL
EC_CAPACITY = 2

VMEM_LIMIT_BYTES = 48 * 1024 * 1024
ROW_TILE = 512
SUBLANES = 8
NA_ROWS_PER_BLOCK = 4
NA_BLOCK_TOKENS = NA_ROWS_PER_BLOCK * GRID_W
GLA_BLOCK = 512
GLA_GROUP = 256
LANES = 128
NOT_SELECTED = -(1 << 24)
MOE_TILE = 256
MOE_SLAB = 48
FFN_TILE = 256
BF16 = jnp.bfloat16
F32 = jnp.float32


def _params(*sem):
    return pltpu.CompilerParams(dimension_semantics=sem, vmem_limit_bytes=VMEM_LIMIT_BYTES)


def _rms(x, g):
    return x * lax.rsqrt(jnp.mean(x * x, axis=-1, keepdims=True) + EPS) * g


def _row_spec(tm, n, col=0):
    return pl.BlockSpec((tm, n), lambda i: (i, col))


def _full_spec(shape):
    return pl.BlockSpec(shape, lambda *_: (0,) * len(shape))


def _mm_body(x_ref, w_ref, o_ref):
    o_ref[...] = jnp.dot(x_ref[...].astype(BF16), w_ref[...],
                         preferred_element_type=F32).astype(o_ref.dtype)


def matmul(x, w, out_dtype):
    m, k = x.shape
    n = w.shape[1]
    tm = min(ROW_TILE, m)
    return pl.pallas_call(
        _mm_body, grid=(m // tm,),
        in_specs=[_row_spec(tm, k), _full_spec((k, n))],
        out_specs=_row_spec(tm, n),
        out_shape=jax.ShapeDtypeStruct((m, n), out_dtype),
        compiler_params=_params("parallel"), name="row_matmul",
    )(x, w)


def _log_sigmoid(x):
    return jnp.minimum(x, 0.0) - jnp.log1p(jnp.exp(-jnp.abs(x)))


def _in_ab_body(x_ref, g_ref, wa_ref, wb_ref, wg_ref, gw_ref, gb_ref, za_ref, zb_ref):
    hn = _rms(x_ref[...], g_ref[...]).astype(BF16)
    za_ref[...] = jnp.dot(hn, wa_ref[...], preferred_element_type=F32).astype(za_ref.dtype)
    nb = wb_ref.shape[1]
    zb_ref[:, :nb] = jnp.dot(hn, wb_ref[...], preferred_element_type=F32)
    lowrank = jnp.dot(hn, wg_ref[...], preferred_element_type=F32)
    hi = lowrank.astype(BF16)
    lo = (lowrank - hi.astype(F32)).astype(BF16)
    pre = jnp.dot(jnp.concatenate([hi, hi, lo], axis=1), gw_ref[...], preferred_element_type=F32) + gb_ref[...]
    zb_ref[:, nb:] = _log_sigmoid(pre) / GLA_TAU


def in_proj_ab(x, gain, w_in, gwf, gbf, gwb, gbb):
    t = x.shape[0]
    tm = ROW_TILE
    na_w = 3 * NA_WIDTH
    gl_w = 2 * GLA_KW + 2 * GLA_VW
    o = np.cumsum([0, NA_WIDTH, NA_WIDTH, NA_WIDTH, GLA_KW, GLA_KW, GLA_VW, GLA_RANK, GLA_RANK, GLA_VW])
    wa = w_in[:, :o[3]].astype(BF16)
    wb = jnp.concatenate([w_in[:, o[3]:o[6]], w_in[:, o[8]:o[9]]], axis=1).astype(BF16)
    wg = w_in[:, o[6]:o[8]].astype(BF16)
    zero = jnp.zeros_like(gwf)
    gw = jnp.concatenate([jnp.concatenate([gwf, zero], axis=1), jnp.concatenate([zero, gwb], axis=1)], axis=0)
    gw_hi = gw.astype(BF16)
    gw_lo = (gw - gw_hi.astype(F32)).astype(BF16)
    gw3 = jnp.concatenate([gw_hi, gw_lo, gw_hi], axis=0)
    gbias = jnp.concatenate([gbf, gbb]).reshape(1, -1)
    return pl.pallas_call(
        _in_ab_body, grid=(t // tm,),
        in_specs=[_row_spec(tm, D_MODEL), _full_spec((1, D_MODEL)), _full_spec((D_MODEL, na_w)),
                  _full_spec((D_MODEL, gl_w)), _full_spec((D_MODEL, 2 * GLA_RANK)),
                  _full_spec((6 * GLA_RANK, 2 * GLA_KW)), _full_spec((1, 2 * GLA_KW))],
        out_specs=[_row_spec(tm, na_w), _row_spec(tm, gl_w + 2 * GLA_KW)],
        out_shape=[jax.ShapeDtypeStruct((t, na_w), BF16),
                   jax.ShapeDtypeStruct((t, gl_w + 2 * GLA_KW), F32)],
        compiler_params=_params("parallel"), name="in_proj_ab",
    )(x, gain.reshape(1, -1), wa, wb, wg, gw3, gbias)


def na_bias_table(rpb):
    rb = NA_ROWS_PER_BLOCK
    a = np.arange(rb)[:, None, None, None]
    c = np.arange(GRID_W)[None, :, None, None]
    u = np.arange(3 * rb)[None, None, :, None]
    kc = np.arange(GRID_W)[None, None, None, :]
    wstart = np.clip(c - NA_WIN_C // 2, 0, GRID_W - NA_WIN_C)
    col_ok = (kc >= wstart) & (kc < wstart + NA_WIN_C)
    dc = np.clip(kc - c + NA_WIN_C - 1, 0, 2 * NA_WIN_C - 2)
    dr = np.clip(u - a + NA_WIN_R // 2 - 1, 0, 2 * NA_WIN_R - 2)
    row_ok = [
        (u >= rb) & (u < rb + NA_WIN_R) & (a >= 0),
        (u - a >= 0) & (u - a < NA_WIN_R),
        (u >= 0) & (u < NA_WIN_R) & (a >= 0),
    ]
    shape = (rb, GRID_W, 3 * rb, GRID_W)
    flat = (NA_BLOCK_TOKENS, 3 * NA_BLOCK_TOKENS)
    pick_r = jnp.asarray(np.eye(2 * NA_WIN_R - 1, dtype=np.float32)[dr[:, 0, :, 0]])
    pick_c = jnp.asarray(np.eye(2 * NA_WIN_C - 1, dtype=np.float32)[dc[0, :, 0, :]])
    vals = jnp.einsum('aud,hdp,ckp->hacuk', pick_r, rpb.astype(F32), pick_c,
                      precision=lax.Precision.HIGHEST).reshape((NA_HEADS,) + flat)
    out = []
    for ok in row_ok:
        m = np.broadcast_to(ok & col_ok, shape).reshape(flat)
        out.append(jnp.where(jnp.asarray(m)[None], vals, NEG_INF))
    return jnp.stack(out)


def _na_body(q_ref, kp_ref, kc_ref, kn_ref, vp_ref, vc_ref, vn_ref, bias_ref, o_ref):
    lane = lax.broadcasted_iota(jnp.int32, (1, 2 * NA_HEAD_DIM), 1)
    scale = NA_HEAD_DIM ** -0.5
    nt = (((1,), (1,)), ((), ()))
    for hp in range(NA_HEADS // 2):
        cols = slice(hp * 2 * NA_HEAD_DIM, (hp + 1) * 2 * NA_HEAD_DIM)
        q = q_ref[:, cols] * scale
        ks = [r[:, cols] for r in (kp_ref, kc_ref, kn_ref)]
        vs = [r[:, cols] for r in (vp_ref, vc_ref, vn_ref)]
        o_pair = None
        for sub in range(2):
            sel = (lane < NA_HEAD_DIM) if sub == 0 else (lane >= NA_HEAD_DIM)
            qm = jnp.where(sel, q, jnp.zeros_like(q))
            s = jnp.concatenate(
                [lax.dot_general(qm, kk, nt, preferred_element_type=F32) for kk in ks], axis=1)
            s = s + bias_ref[0, 2 * hp + sub]
            p = jnp.exp(s - jnp.max(s, axis=-1, keepdims=True))
            l = jnp.sum(p, axis=-1, keepdims=True)
            pb = p.astype(BF16)
            o = None
            for j, vv in enumerate(vs):
                t = jnp.dot(pb[:, j * NA_BLOCK_TOKENS:(j + 1) * NA_BLOCK_TOKENS], vv,
                            preferred_element_type=F32)
                o = t if o is None else o + t
            o = o / l
            o_pair = o if o_pair is None else jnp.where(sel, o, o_pair)
        o_ref[:, cols] = o_pair.astype(o_ref.dtype)


def neighbourhood_attention(za, rpb, batch, n):
    t = za.shape[0]
    bt = NA_BLOCK_TOKENS
    nblk = n // bt
    assert n % bt == 0 and nblk >= 3
    bias = na_bias_table(rpb)

    def qmap(b, i):
        return (b * nblk + i, 0)

    def kmap(d, col):
        return lambda b, i: (b * nblk + jnp.clip(i + d, 0, nblk - 1), col)

    def bmap(b, i):
        return (jnp.where(i == 0, 0, jnp.where(i == nblk - 1, 2, 1)), 0, 0, 0)

    def blk(m):
        return pl.BlockSpec((bt, NA_WIDTH), m)

    return pl.pallas_call(
        _na_body, grid=(batch, nblk),
        in_specs=[blk(qmap), blk(kmap(-1, 1)), blk(kmap(0, 1)), blk(kmap(1, 1)),
                  blk(kmap(-1, 2)), blk(kmap(0, 2)), blk(kmap(1, 2)),
                  pl.BlockSpec((1, NA_HEADS, bt, 3 * bt), bmap)],
        out_specs=blk(qmap),
        out_shape=jax.ShapeDtypeStruct((t, NA_WIDTH), BF16),
        compiler_params=_params("parallel", "parallel"), name="neighbourhood_attention",
    )(za, za, za, za, za, za, za, bias)


def _split_dot(a_bf, x, dims):
    hi = x.astype(BF16)
    lo = (x - hi.astype(F32)).astype(BF16)
    return (lax.dot_general(a_bf, hi, dims, preferred_element_type=F32)
            + lax.dot_general(a_bf, lo, dims, preferred_element_type=F32))


def _split_dot_t(x, a_bf, dims):
    hi = x.astype(BF16)
    lo = (x - hi.astype(F32)).astype(BF16)
    return (lax.dot_general(hi, a_bf, dims, preferred_element_type=F32)
            + lax.dot_general(lo, a_bf, dims, preferred_element_type=F32))


def _gla_body(qf_ref, kf_ref, vf_ref, gf_ref, qb_ref, kb_ref, vb_ref, gb_ref,
              of_ref, ob_ref, sf_ref, sb_ref):
    c = GLA_CHUNK
    blk = GLA_BLOCK
    grp = GLA_GROUP
    nchunk = blk // c

    @pl.when(pl.program_id(1) == 0)
    def _():
        sf_ref[...] = jnp.zeros_like(sf_ref)
        sb_ref[...] = jnp.zeros_like(sb_ref)

    row = lax.broadcasted_iota(jnp.int32, (blk, blk), 0)
    col = lax.broadcasted_iota(jnp.int32, (blk, blk), 1)
    same_chunk = (row // c) == (col // c)
    grow = lax.broadcasted_iota(jnp.int32, (grp, grp), 0)
    gcol = lax.broadcasted_iota(jnp.int32, (grp, grp), 1)
    same_chunk_g = (grow // c) == (gcol // c)
    mm = (((1,), (0,)), ((), ()))
    nt = (((1,), (1,)), ((), ()))
    tn = (((0,), (0,)), ((), ()))

    def prepare(q_ref, k_ref, v_ref, g_ref, reverse):
        keep_blk = same_chunk & ((row <= col) if reverse else (row >= col))
        keep_grp = same_chunk_g & ((grow <= gcol) if reverse else (grow >= gcol))
        g = g_ref[...]
        cum = _split_dot(keep_blk.astype(BF16), g, mm)
        edge = 0 if reverse else c - 1
        tot = cum.reshape(nchunk, c, GLA_KW)[:, edge:edge + 1, :]
        tot_rows = jnp.broadcast_to(tot, (nchunk, c, GLA_KW)).reshape(blk, GLA_KW)
        q = q_ref[...] * (GLA_DK ** -0.5)
        k = k_ref[...]
        q_t = (q * jnp.exp(cum)).astype(BF16)
        k_t = (k * jnp.exp(-cum)).astype(BF16)
        k_d = (k * jnp.exp(tot_rows - cum)).astype(BF16)
        v = v_ref[...].astype(BF16)
        intra = []
        for gi in range(blk // grp):
            rows = slice(gi * grp, (gi + 1) * grp)
            per_head = []
            for h in range(GLA_HEADS):
                kc = slice(h * GLA_DK, (h + 1) * GLA_DK)
                vc = slice(h * GLA_DV, (h + 1) * GLA_DV)
                a = lax.dot_general(q_t[rows, kc], k_t[rows, kc], nt, preferred_element_type=F32)
                a = jnp.where(keep_grp, a, 0.0).astype(BF16)
                per_head.append(jnp.dot(a, v[rows, vc], preferred_element_type=F32))
            intra.append(jnp.concatenate(per_head, axis=1))
        return jnp.concatenate(intra, axis=0), q_t, k_d, v, jnp.exp(tot)

    dirs = [(prepare(qf_ref, kf_ref, vf_ref, gf_ref, False), of_ref, sf_ref, False),
            (prepare(qb_ref, kb_ref, vb_ref, gb_ref, True), ob_ref, sb_ref, True)]
    states = [[s_ref[h] for h in range(GLA_HEADS)] for _, _, s_ref, _ in dirs]
    for step in range(nchunk):
        for d, ((intra, q_t, k_d, v, decay), o_ref, _, reverse) in enumerate(dirs):
            j = nchunk - 1 - step if reverse else step
            rows = slice(j * c, (j + 1) * c)
            outs = []
            for h in range(GLA_HEADS):
                kc = slice(h * GLA_DK, (h + 1) * GLA_DK)
                vc = slice(h * GLA_DV, (h + 1) * GLA_DV)
                st = states[d][h]
                outs.append(lax.dot_general(q_t[rows, kc], st.astype(BF16), nt, preferred_element_type=F32))
                ds = lax.dot_general(v[rows, vc], k_d[rows, kc], tn, preferred_element_type=F32)
                states[d][h] = st * decay[j, :, kc] + ds
            o_ref[rows, :] = intra[rows, :] + jnp.concatenate(outs, axis=1)
    for d, (_, _, s_ref, _) in enumerate(dirs):
        for h in range(GLA_HEADS):
            s_ref[h] = states[d][h]


def gla_bidirectional(zb, batch, n):
    t = zb.shape[0]
    nb = n // GLA_BLOCK
    assert n % GLA_BLOCK == 0

    def fwd(col):
        return lambda b, i: (b * nb + i, col)

    def bwd(col):
        return lambda b, i: (b * nb + nb - 1 - i, col)

    def kw(m):
        return pl.BlockSpec((GLA_BLOCK, GLA_KW), m)

    def vw(m):
        return pl.BlockSpec((GLA_BLOCK, GLA_VW), m)

    return pl.pallas_call(
        _gla_body, grid=(batch, nb),
        in_specs=[kw(fwd(0)), kw(fwd(1)), vw(fwd(1)), kw(fwd(6)),
                  kw(bwd(0)), kw(bwd(1)), vw(bwd(1)), kw(bwd(7))],
        out_specs=[vw(fwd(0)), vw(bwd(0))],
        out_shape=[jax.ShapeDtypeStruct((t, GLA_VW), F32)] * 2,
        scratch_shapes=[pltpu.VMEM((GLA_HEADS, GLA_DV, GLA_DK), F32)] * 2,
        compiler_params=_params("parallel", "arbitrary"), name="gla_bidirectional",
    )(zb, zb, zb, zb, zb, zb, zb, zb)


def _silu(x):
    return x * jax.nn.sigmoid(x)


def _out_ab_body(oa_ref, of_ref, ob_ref, r_ref, gn_ref, wa_ref, wb_ref, x_ref, o_ref):
    o = of_ref[...] + ob_ref[...]
    r = r_ref[...]
    parts = []
    for h in range(GLA_HEADS):
        vc = slice(h * GLA_DV, (h + 1) * GLA_DV)
        parts.append((_rms(o[:, vc], gn_ref[...]) * _silu(r[:, vc])).astype(BF16))
    o_b = jnp.concatenate(parts, axis=1)
    o_ref[...] = (jnp.dot(oa_ref[...], wa_ref[...], preferred_element_type=F32)
                  + jnp.dot(o_b, wb_ref[...], preferred_element_type=F32) + x_ref[...])


def out_proj_ab(o_a, o_fw, o_bw, zb, gla_norm, w_out, x):
    t = x.shape[0]
    tm = ROW_TILE
    w = w_out.astype(BF16)
    return pl.pallas_call(
        _out_ab_body, grid=(t // tm,),
        in_specs=[_row_spec(tm, NA_WIDTH), _row_spec(tm, GLA_VW), _row_spec(tm, GLA_VW),
                  _row_spec(tm, GLA_VW, col=2), _full_spec((1, GLA_DV)),
                  _full_spec((NA_WIDTH, D_MODEL)), _full_spec((GLA_VW, D_MODEL)),
                  _row_spec(tm, D_MODEL)],
        out_specs=_row_spec(tm, D_MODEL),
        out_shape=jax.ShapeDtypeStruct((t, D_MODEL), F32),
        compiler_params=_params("parallel"), name="out_proj_ab",
    )(o_a, o_fw, o_bw, zb, gla_norm.reshape(1, -1), w[:NA_WIDTH], w[NA_WIDTH:], x)


def _in_c_body(x_ref, g_ref, wbg_ref, wcg_ref, wxt_ref, bg_ref, u_ref):
    hn = _rms(x_ref[...], g_ref[...]).astype(BF16)
    bg_ref[...] = jnp.dot(hn, wbg_ref[...], preferred_element_type=F32)
    u_ref[...] = (jnp.dot(hn, wcg_ref[...], preferred_element_type=F32)
                  * jnp.dot(hn, wxt_ref[...], preferred_element_type=F32))


def in_proj_c(x, gain, w_in):
    t = x.shape[0]
    tm = ROW_TILE
    w = w_in.astype(BF16)
    d = D_MODEL
    return pl.pallas_call(
        _in_c_body, grid=(t // tm,),
        in_specs=[_row_spec(tm, d), _full_spec((1, d))] + [_full_spec((d, d))] * 3,
        out_specs=[_row_spec(tm, d)] * 2,
        out_shape=[jax.ShapeDtypeStruct((t, d), F32)] * 2,
        compiler_params=_params("parallel"), name="in_proj_c",
    )(x, gain.reshape(1, -1), w[:, :d], w[:, d:2 * d], w[:, 2 * d:])


def _conv_out_body(u_ref, up_ref, un_ref, bg_ref, cw_ref, w_ref, x_ref, o_ref, *, tiles_per_seq):
    i = pl.program_id(0)
    tm = u_ref.shape[0]
    u = u_ref[...]
    row = lax.broadcasted_iota(jnp.int32, (tm, 1), 0)
    first = (i % tiles_per_seq) == 0
    last = (i % tiles_per_seq) == tiles_per_seq - 1
    prev_row = jnp.where(first, 0.0, up_ref[SUBLANES - 1:SUBLANES, :])
    next_row = jnp.where(last, 0.0, un_ref[0:1, :])
    u_prev = jnp.where(row == 0, prev_row, pltpu.roll(u, 1, axis=0))
    u_next = jnp.where(row == tm - 1, next_row, pltpu.roll(u, tm - 1, axis=0))
    conv = cw_ref[0:1, :] * u_prev + cw_ref[1:2, :] * u + cw_ref[2:3, :] * u_next
    o_ref[...] = (jnp.dot((bg_ref[...] * conv).astype(BF16), w_ref[...], preferred_element_type=F32)
                  + x_ref[...])


def conv_out_proj(u, bg, conv_w, w_out, x, n):
    t = x.shape[0]
    tm = ROW_TILE
    d = D_MODEL
    per = tm // SUBLANES
    nhalo = t // SUBLANES
    assert n % tm == 0
    return pl.pallas_call(
        functools.partial(_conv_out_body, tiles_per_seq=n // tm), grid=(t // tm,),
        in_specs=[_row_spec(tm, d),
                  pl.BlockSpec((SUBLANES, d), lambda i: (jnp.maximum(i * per - 1, 0), 0)),
                  pl.BlockSpec((SUBLANES, d), lambda i: (jnp.minimum((i + 1) * per, nhalo - 1), 0)),
                  _row_spec(tm, d), _full_spec((CONV_W, d)), _full_spec((d, d)), _row_spec(tm, d)],
        out_specs=_row_spec(tm, d),
        out_shape=jax.ShapeDtypeStruct((t, d), F32),
        compiler_params=_params("parallel"), name="conv_out_proj",
    )(u, u, u, bg, conv_w, w_out.astype(BF16), x)


def _mem_body(x_ref, g_ref, wq_ref, k_ref, v_ref, wo_ref, o_ref):
    x = x_ref[...]
    hn = _rms(x, g_ref[...]).astype(BF16)
    q = (jnp.dot(hn, wq_ref[...], preferred_element_type=F32) * MEM_HEAD_DIM ** -0.5).astype(BF16)
    nt = (((1,), (1,)), ((), ()))
    parts = []
    for h in range(MEM_HEADS):
        hc = slice(h * MEM_HEAD_DIM, (h + 1) * MEM_HEAD_DIM)
        s = lax.dot_general(q[:, hc], k_ref[0, :, hc], nt, preferred_element_type=F32)
        p = jnp.exp(s - jnp.max(s, axis=-1, keepdims=True))
        l = jnp.sum(p, axis=-1, keepdims=True)
        o = jnp.dot(p.astype(BF16), v_ref[0, :, hc], preferred_element_type=F32) / l
        parts.append(o.astype(BF16))
    o_ref[...] = jnp.dot(jnp.concatenate(parts, axis=1), wo_ref[...], preferred_element_type=F32) + x


def memory_attention(x, mem, gain, wq, wk, wv, wo, n):
    t, d = x.shape
    b, nm, _ = mem.shape
    tm = ROW_TILE
    per = n // tm
    memf = mem.reshape(b * nm, d)
    k = matmul(memf, wk.astype(BF16), BF16).reshape(b, nm, d)
    v = matmul(memf, wv.astype(BF16), BF16).reshape(b, nm, d)
    kv_spec = pl.BlockSpec((1, nm, d), lambda i: (i // per, 0, 0))
    return pl.pallas_call(
        _mem_body, grid=(t // tm,),
        in_specs=[_row_spec(tm, d), _full_spec((1, d)), _full_spec((d, d)), kv_spec, kv_spec,
                  _full_spec((d, d))],
        out_specs=_row_spec(tm, d),
        out_shape=jax.ShapeDtypeStruct((t, d), F32),
        compiler_params=_params("parallel"), name="memory_attention",
    )(x, gain.reshape(1, -1), wq.astype(BF16), k, v, wo.astype(BF16))


def _router_body(x_ref, g_ref, r_ref, aff_ref):
    hn = _rms(x_ref[...], g_ref[...])
    r = r_ref[...]
    hn_hi, r_hi = hn.astype(BF16), r.astype(BF16)
    hn_lo, r_lo = (hn - hn_hi.astype(F32)).astype(BF16), (r - r_hi.astype(F32)).astype(BF16)
    nt = (((1,), (1,)), ((), ()))
    logits = (lax.dot_general(r_hi, hn_hi, nt, preferred_element_type=F32)
              + lax.dot_general(r_hi, hn_lo, nt, preferred_element_type=F32)
              + lax.dot_general(r_lo, hn_hi, nt, preferred_element_type=F32))
    p = jnp.exp(logits - jnp.max(logits, axis=0, keepdims=True))
    aff_ref[...] = p / jnp.sum(p, axis=0, keepdims=True)


def router_affinity(x, gain, router):
    t, d = x.shape
    tm = ROW_TILE
    return pl.pallas_call(
        _router_body, grid=(t // tm,),
        in_specs=[_row_spec(tm, d), _full_spec((1, d)), _full_spec((N_EXPERTS, d))],
        out_specs=pl.BlockSpec((N_EXPERTS, tm), lambda i: (0, i)),
        out_shape=jax.ShapeDtypeStruct((N_EXPERTS, t), F32),
        compiler_params=_params("parallel"), name="router_affinity",
    )(x, gain.reshape(1, -1), router.T)


def _select_body(aff_ref, slot_ref, base_ref, cnt_ref, *, cap):
    e, nc, _ = aff_ref.shape
    per = MOE_TILE // LANES
    bits = pltpu.bitcast(aff_ref[...], jnp.int32)

    def count(mask):
        s = jnp.sum(mask.astype(jnp.int32), axis=1, keepdims=True)
        return jnp.sum(s, axis=2, keepdims=True)

    def bisect(b, thr):
        cand = thr | (jnp.int32(1) << (30 - b))
        return jnp.where(count(bits >= cand) >= cap, cand, thr)

    thr = lax.fori_loop(0, 31, bisect, jnp.zeros((e, 1, 1), jnp.int32))
    gt = bits > thr
    eq = bits == thr
    need_eq = cap - count(gt)

    li = lax.broadcasted_iota(jnp.int32, (LANES, LANES), 0)
    lj = lax.broadcasted_iota(jnp.int32, (LANES, LANES), 1)
    incl = (li <= lj).astype(BF16)
    ci = lax.broadcasted_iota(jnp.int32, (nc, nc), 0)
    cj = lax.broadcasted_iota(jnp.int32, (nc, nc), 1)
    strict = (cj < ci).astype(BF16)
    same_tile = (cj // per == ci // per).astype(BF16)
    same_tile_before = ((cj // per == ci // per) & (cj < ci)).astype(BF16)
    tiles_before = ((cj // per < ci // per) & (cj % per == 0)).astype(BF16)

    def chunk_prefix(m):
        within = jnp.dot(m, incl, preferred_element_type=F32)
        tot = jnp.broadcast_to(within[:, LANES - 1:LANES], (nc, LANES)).astype(BF16)
        return within, tot

    for x in range(e):
        eq_x = eq[x].astype(BF16)
        within, tot = chunk_prefix(eq_x)
        rank_eq = within - eq_x.astype(F32) + jnp.dot(strict, tot, preferred_element_type=F32)
        sel = gt[x] | (eq[x] & (rank_eq < need_eq[x].astype(F32)))
        sel_b = sel.astype(BF16)
        within, tot = chunk_prefix(sel_b)
        tile_cnt = jnp.dot(same_tile, tot, preferred_element_type=F32).astype(jnp.int32)
        padded = ((tile_cnt + (SUBLANES - 1)) // SUBLANES) * SUBLANES
        base = jnp.dot(tiles_before, padded.astype(BF16), preferred_element_type=F32)
        in_tile = jnp.dot(same_tile_before, tot, preferred_element_type=F32)
        pos = (base + in_tile + within - sel_b.astype(F32)).astype(jnp.int32)
        slot_ref[x] = jnp.where(sel, pos, NOT_SELECTED)
        base_ref[x] = base.astype(jnp.int32)
        cnt_ref[x] = tile_cnt


def select_slots(aff_t, cap):
    e, t = aff_t.shape
    nc = t // LANES
    per = MOE_TILE // LANES
    slot, base, cnt = pl.pallas_call(
        functools.partial(_select_body, cap=cap),
        out_shape=[jax.ShapeDtypeStruct((e, nc, LANES), jnp.int32)] * 3,
        compiler_params=pltpu.CompilerParams(vmem_limit_bytes=VMEM_LIMIT_BYTES), name="select_slots",
    )(aff_t.reshape(e, nc, LANES))
    base = base[:, ::per, 0]
    cnt = cnt[:, ::per, 0]
    rows = base[:, -1] + ((cnt[:, -1] + SUBLANES - 1) // SUBLANES) * SUBLANES
    return slot.reshape(e, t), base.reshape(-1), cnt.reshape(-1), rows


def slab_rows(t, cap):
    worst = cap + (SUBLANES - 1) * (t // MOE_TILE)
    return -(-worst // FFN_TILE) * FFN_TILE + FFN_TILE


def _one_hot_t(slot_ref, lo, start, width):
    tm = slot_ref.shape[1]
    w = lax.broadcasted_iota(jnp.int32, (width, tm), 0)
    rows = []
    for e in range(N_EXPERTS):
        s = slot_ref[e:e + 1, :]
        rows.append((s - start[e] == w) & (s >= lo[e]) & (s < lo[e] + width))
    return jnp.concatenate(rows, axis=0)


def _tile_scalars(base_ref, cnt_ref, i, nt):
    lo = [base_ref[e * nt + i] for e in range(N_EXPERTS)]
    cnt = [cnt_ref[e * nt + i] for e in range(N_EXPERTS)]
    return lo, cnt


def _rounds(cnt):
    m = cnt[0]
    for c in cnt[1:]:
        m = jnp.maximum(m, c)
    return (m + MOE_SLAB - 1) // MOE_SLAB


def _dispatch_body(base_ref, cnt_ref, rows_ref, x_ref, g_ref, slot_ref, aff_ref, xe_ref, ge_ref,
                   xbuf, gbuf, xover, gover, xzero, gzero, sem, osem, zsem, *, nt):
    i = pl.program_id(0)
    cur = i % 2
    w_ = MOE_SLAB
    hn = _rms(x_ref[...], g_ref[...]).astype(BF16)
    lo, cnt = _tile_scalars(base_ref, cnt_ref, i, nt)

    def stage(xdst, gdst, lo_k):
        hot = _one_hot_t(slot_ref, lo_k, lo_k, w_)
        rows = jnp.dot(hot.astype(BF16), hn, preferred_element_type=F32)
        hot_f = hot.astype(F32)
        for e in range(N_EXPERTS):
            sl = slice(e * w_, (e + 1) * w_)
            xdst[e] = rows[sl]
            gate = jnp.sum(hot_f[sl] * aff_ref[e:e + 1, :], axis=1, keepdims=True)
            gdst[e] = jnp.broadcast_to(gate, (w_, LANES))

    def copies(xsrc, gsrc, lo_k, e, xs, gs):
        dst = pl.ds(pl.multiple_of(lo_k[e], SUBLANES), w_)
        return (pltpu.make_async_copy(xsrc.at[e], xe_ref.at[e, dst], xs),
                pltpu.make_async_copy(gsrc.at[e], ge_ref.at[e, dst], gs))

    stage(xbuf.at[cur], gbuf.at[cur], lo)

    @pl.when(i > 0)
    def _():
        lo_p, _ = _tile_scalars(base_ref, cnt_ref, i - 1, nt)
        for e in range(N_EXPERTS):
            for cp in copies(xbuf.at[1 - cur], gbuf.at[1 - cur], lo_p, e, sem.at[1 - cur, 0], sem.at[1 - cur, 1]):
                cp.wait()

    for e in range(N_EXPERTS):
        for cp in copies(xbuf.at[cur], gbuf.at[cur], lo, e, sem.at[cur, 0], sem.at[cur, 1]):
            cp.start()

    def extra_round(k, carry):
        lo_k = [l + k * w_ for l in lo]
        stage(xover, gover, lo_k)
        for e in range(N_EXPERTS):
            @pl.when(cnt[e] > k * w_)
            def _():
                for cp in copies(xover, gover, lo_k, e, osem.at[0], osem.at[1]):
                    cp.start()
        for e in range(N_EXPERTS):
            @pl.when(cnt[e] > k * w_)
            def _():
                for cp in copies(xover, gover, lo_k, e, osem.at[0], osem.at[1]):
                    cp.wait()
        return carry

    lax.fori_loop(1, _rounds(cnt), extra_round, 0)

    @pl.when(i == nt - 1)
    def _():
        for e in range(N_EXPERTS):
            for cp in copies(xbuf.at[cur], gbuf.at[cur], lo, e, sem.at[cur, 0], sem.at[cur, 1]):
                cp.wait()
        xzero[...] = jnp.zeros_like(xzero)
        gzero[...] = jnp.zeros_like(gzero)

        def tail(e):
            dst = pl.ds(pl.multiple_of(rows_ref[e], SUBLANES), FFN_TILE)
            return (pltpu.make_async_copy(xzero, xe_ref.at[e, dst], zsem.at[0]),
                    pltpu.make_async_copy(gzero, ge_ref.at[e, dst], zsem.at[1]))

        for e in range(N_EXPERTS):
            for cp in tail(e):
                cp.start()
        for e in range(N_EXPERTS):
            for cp in tail(e):
                cp.wait()


def dispatch(x, gain, slot, aff_t, base, cnt, rows, rows_alloc):
    t, d = x.shape
    tm = MOE_TILE
    nt = t // tm
    any_spec = pl.BlockSpec(memory_space=pl.ANY)
    grid_spec = pltpu.PrefetchScalarGridSpec(
        num_scalar_prefetch=3, grid=(nt,),
        in_specs=[pl.BlockSpec((tm, d), lambda i, *_: (i, 0)), pl.BlockSpec((1, d), lambda i, *_: (0, 0)),
                  pl.BlockSpec((N_EXPERTS, tm), lambda i, *_: (0, i)),
                  pl.BlockSpec((N_EXPERTS, tm), lambda i, *_: (0, i))],
        out_specs=[any_spec, any_spec],
        scratch_shapes=[pltpu.VMEM((2, N_EXPERTS, MOE_SLAB, d), F32),
                        pltpu.VMEM((2, N_EXPERTS, MOE_SLAB, LANES), F32),
                        pltpu.VMEM((N_EXPERTS, MOE_SLAB, d), F32),
                        pltpu.VMEM((N_EXPERTS, MOE_SLAB, LANES), F32),
                        pltpu.VMEM((FFN_TILE, d), F32), pltpu.VMEM((FFN_TILE, LANES), F32),
                        pltpu.SemaphoreType.DMA((2, 2)), pltpu.SemaphoreType.DMA((2,)),
                        pltpu.SemaphoreType.DMA((2,))])
    return pl.pallas_call(
        functools.partial(_dispatch_body, nt=nt), grid_spec=grid_spec,
        out_shape=[jax.ShapeDtypeStruct((N_EXPERTS, rows_alloc, d), F32),
                   jax.ShapeDtypeStruct((N_EXPERTS, rows_alloc, LANES), F32)],
        compiler_params=_params("arbitrary"), name="moe_dispatch",
    )(base, cnt, rows, x, gain.reshape(1, -1), slot, aff_t)


def _ffn_body(rows_ref, xe_ref, gate_ref, wg_ref, wu_ref, wd_ref, o_ref):
    used = pl.program_id(1) * FFN_TILE < rows_ref[pl.program_id(0)]

    @pl.when(jnp.logical_not(used))
    def _():
        o_ref[...] = jnp.zeros_like(o_ref)

    @pl.when(used)
    def _():
        xe = xe_ref[0].astype(BF16)
        a = jnp.dot(xe, wg_ref[0], preferred_element_type=F32)
        b = jnp.dot(xe, wu_ref[0], preferred_element_type=F32)
        hid = (_silu(a) * b).astype(BF16)
        o_ref[0] = jnp.dot(hid, wd_ref[0], preferred_element_type=F32) * gate_ref[0][:, :1]


def expert_ffn(xe, gate, wg, wu, wd, rows):
    e, rows_alloc, d = xe.shape
    f = wg.shape[-1]
    tm = FFN_TILE
    grid_spec = pltpu.PrefetchScalarGridSpec(
        num_scalar_prefetch=1, grid=(e, rows_alloc // tm - 1),
        in_specs=[
            pl.BlockSpec((1, tm, d), lambda i, j, r: (i, jnp.minimum(j, (r[i] - 1) // tm), 0)),
            pl.BlockSpec((1, tm, LANES), lambda i, j, r: (i, jnp.minimum(j, (r[i] - 1) // tm), 0)),
            pl.BlockSpec((1, d, f), lambda i, j, *_: (i, 0, 0)),
            pl.BlockSpec((1, d, f), lambda i, j, *_: (i, 0, 0)),
            pl.BlockSpec((1, f, d), lambda i, j, *_: (i, 0, 0)),
        ],
        out_specs=pl.BlockSpec((1, tm, d), lambda i, j, *_: (i, j, 0)))
    return pl.pallas_call(
        _ffn_body, grid_spec=grid_spec,
        out_shape=jax.ShapeDtypeStruct((e, rows_alloc - tm, d), F32),
        compiler_params=_params("parallel", "parallel"), name="expert_ffn",
    )(rows, xe, gate, wg, wu, wd)


def _combine_body(base_ref, cnt_ref, rows_ref, x_ref, slot_ref, y_ref, o_ref, ybuf, yover, sem, osem, *, nt):
    i = pl.program_id(0)
    cur = i % 2
    w_ = MOE_SLAB
    tn = (((0,), (0,)), ((), ()))

    def window(lo_k):
        return [pl.multiple_of(jnp.minimum(lo_k[e], rows_ref[e] - w_), SUBLANES) for e in range(N_EXPERTS)]

    def fetch(step, buf, xs):
        lo_s, _ = _tile_scalars(base_ref, cnt_ref, step, nt)
        st = window(lo_s)
        return [pltpu.make_async_copy(y_ref.at[e, pl.ds(st[e], w_)], buf.at[e], xs)
                for e in range(N_EXPERTS)]

    @pl.when(i == 0)
    def _():
        yover[...] = jnp.zeros_like(yover)
        for cp in fetch(0, ybuf.at[0], sem.at[0]):
            cp.start()

    @pl.when(i + 1 < nt)
    def _():
        for cp in fetch(i + 1, ybuf.at[1 - cur], sem.at[1 - cur]):
            cp.start()

    for cp in fetch(i, ybuf.at[cur], sem.at[cur]):
        cp.wait()

    lo, cnt = _tile_scalars(base_ref, cnt_ref, i, nt)

    def scatter(buf, lo_k):
        hot = _one_hot_t(slot_ref, lo_k, window(lo_k), w_).astype(BF16)
        ys = buf[...].reshape(N_EXPERTS * w_, buf.shape[-1]).astype(BF16)
        return lax.dot_general(hot, ys, tn, preferred_element_type=F32)

    o_ref[...] = x_ref[...] + scatter(ybuf.at[cur], lo)

    def extra_round(k, carry):
        lo_k = [l + k * w_ for l in lo]
        st = window(lo_k)
        for e in range(N_EXPERTS):
            @pl.when(cnt[e] > k * w_)
            def _():
                pltpu.make_async_copy(y_ref.at[e, pl.ds(st[e], w_)], yover.at[e], osem).start()
        for e in range(N_EXPERTS):
            @pl.when(cnt[e] > k * w_)
            def _():
                pltpu.make_async_copy(y_ref.at[e, pl.ds(st[e], w_)], yover.at[e], osem).wait()
        o_ref[...] += scatter(yover, lo_k)
        return carry

    lax.fori_loop(1, _rounds(cnt), extra_round, 0)


def combine(x, slot, y, base, cnt, rows):
    t, d = x.shape
    tm = MOE_TILE
    nt = t // tm
    grid_spec = pltpu.PrefetchScalarGridSpec(
        num_scalar_prefetch=3, grid=(nt,),
        in_specs=[pl.BlockSpec((tm, d), lambda i, *_: (i, 0)),
                  pl.BlockSpec((N_EXPERTS, tm), lambda i, *_: (0, i)),
                  pl.BlockSpec(memory_space=pl.ANY)],
        out_specs=pl.BlockSpec((tm, d), lambda i, *_: (i, 0)),
        scratch_shapes=[pltpu.VMEM((2, N_EXPERTS, MOE_SLAB, d), F32),
                        pltpu.VMEM((N_EXPERTS, MOE_SLAB, d), F32),
                        pltpu.SemaphoreType.DMA((2,)), pltpu.SemaphoreType.DMA(())])
    return pl.pallas_call(
        functools.partial(_combine_body, nt=nt), grid_spec=grid_spec,
        out_shape=jax.ShapeDtypeStruct((t, d), F32),
        compiler_params=_params("arbitrary"), name="moe_combine",
    )(base, cnt, rows, x, slot, y)


def _norm_body(x_ref, g_ref, o_ref):
    o_ref[...] = _rms(x_ref[...], g_ref[...]).astype(o_ref.dtype)


def rms_norm_rows(x, gain, out_dtype):
    t, d = x.shape
    tm = ROW_TILE
    return pl.pallas_call(
        _norm_body, grid=(t // tm,),
        in_specs=[_row_spec(tm, d), _full_spec((1, d))],
        out_specs=_row_spec(tm, d),
        out_shape=jax.ShapeDtypeStruct((t, d), out_dtype),
        compiler_params=_params("parallel"), name="rms_norm_rows",
    )(x, gain.reshape(1, -1))


def expert_choice_ffn(x, layer, p):
    t, _ = x.shape
    gain = p['norm_ffn'][layer]
    cap = EC_CAPACITY * t // N_EXPERTS
    aff_t = router_affinity(x, gain, p['router'][layer])
    slot, base, cnt, rows = select_slots(aff_t, cap)
    xe, ge = dispatch(x, gain, slot, aff_t, base, cnt, rows, slab_rows(t, cap))
    y = expert_ffn(xe, ge, p['w_gate'][layer], p['w_up'][layer], p['w_down'][layer], rows)
    return combine(x, slot, y, base, cnt, rows)


def trunk(x, mem, p):
    b, n, d = x.shape
    x = x.reshape(b * n, d)
    for layer in range(DEPTH):
        i = layer // 2
        if layer % 2 == 0:
            za, zb = in_proj_ab(x, p['norm_mix'][layer], p['w_in_ab'][i], p['gla_gw_f'][i],
                                p['gla_gb_f'][i], p['gla_gw_b'][i], p['gla_gb_b'][i])
            o_a = neighbourhood_attention(za, p['na_rpb'][i], b, n)
            o_fw, o_bw = gla_bidirectional(zb, b, n)
            x = out_proj_ab(o_a, o_fw, o_bw, zb, p['gla_norm'][i], p['w_out_ab'][i], x)
        else:
            bg, u = in_proj_c(x, p['norm_mix'][layer], p['w_in_c'][i])
            x = conv_out_proj(u, bg, p['conv_w'][i], p['w_out_c'][i], x, n)
        x = memory_attention(x, mem, p['norm_mem'][layer], p['w_mq'][layer], p['w_mk'][layer],
                             p['w_mv'][layer], p['w_mo'][layer], n)
        x = expert_choice_ffn(x, layer, p)
    return rms_norm_rows(x, p['final_norm'], F32).reshape(b, n, d)


def kernel(x_prompt, x_sample, mem_prompt, mem_sample, w_in_ab, na_rpb, gla_gw_f, gla_gb_f, gla_gw_b,
           gla_gb_b, gla_norm, w_out_ab, w_in_c, conv_w, w_out_c, norm_mix, norm_mem, norm_ffn,
           w_mq, w_mk, w_mv, w_mo, router, w_gate, w_up, w_down, final_norm):
    p = dict(w_in_ab=w_in_ab, na_rpb=na_rpb, gla_gw_f=gla_gw_f, gla_gb_f=gla_gb_f, gla_gw_b=gla_gw_b,
             gla_gb_b=gla_gb_b, gla_norm=gla_norm, w_out_ab=w_out_ab, w_in_c=w_in_c, conv_w=conv_w,
             w_out_c=w_out_c, norm_mix=norm_mix, norm_mem=norm_mem, norm_ffn=norm_ffn, w_mq=w_mq,
             w_mk=w_mk, w_mv=w_mv, w_mo=w_mo, router=router, w_gate=w_gate.astype(BF16),
             w_up=w_up.astype(BF16), w_down=w_down.astype(BF16), final_norm=final_norm)
    y_prompt = trunk(x_prompt, mem_prompt, p)
    y_sample = trunk(x_sample, mem_sample, p)
    return (y_prompt, y_sample)
```

```python
import functools

import jax
import jax.numpy as jnp
import numpy as np
from jax import lax
from jax.experimental import pallas as pl
from jax.experimental.pallas import tpu as pltpu

D_MODEL = 1024
DEPTH = 4
GRID_W = 64
EPS = 1e-6
NEG_INF = -1e30
NA_HEADS = 8
NA_HEAD_DIM = 64
NA_WIN_R = 8
NA_WIN_C = 16
NA_WIDTH = NA_HEADS * NA_HEAD_DIM
GLA_HEADS = 4
GLA_DK = 64
GLA_DV = 128
GLA_RANK = 16
GLA_TAU = 16.0
GLA_CHUNK = 64
GLA_KW = GLA_HEADS * GLA_DK
GLA_VW = GLA_HEADS * GLA_DV
CONV_W = 3
MEM_HEADS = 4
MEM_HEAD_DIM = D_MODEL // MEM_HEADS
N_EXPERTS = 16
D_EXPERT = 2 * D_MODEL
EC_CAPACITY = 2

VMEM_LIMIT_BYTES = 48 * 1024 * 1024
ROW_TILE = 512
SUBLANES = 8
NA_ROWS_PER_BLOCK = 4
NA_BLOCK_TOKENS = NA_ROWS_PER_BLOCK * GRID_W
GLA_BLOCK = 512
GLA_GROUP = 256
LANES = 128
NOT_SELECTED = -(1 << 24)
MOE_TILE = 256
MOE_SLAB = 64
SLAB_ALIGN = 16
FFN_TILE = 256
BF16 = jnp.bfloat16
F32 = jnp.float32


def _params(*sem):
    return pltpu.CompilerParams(dimension_semantics=sem, vmem_limit_bytes=VMEM_LIMIT_BYTES)


def _rms(x, g):
    return x * lax.rsqrt(jnp.mean(x * x, axis=-1, keepdims=True) + EPS) * g


def _row_spec(tm, n, col=0):
    return pl.BlockSpec((tm, n), lambda i: (i, col))


def _full_spec(shape):
    return pl.BlockSpec(shape, lambda *_: (0,) * len(shape))


def _mm_body(x_ref, w_ref, o_ref):
    o_ref[...] = jnp.dot(x_ref[...].astype(BF16), w_ref[...],
                         preferred_element_type=F32).astype(o_ref.dtype)


def matmul(x, w, out_dtype):
    m, k = x.shape
    n = w.shape[1]
    tm = min(ROW_TILE, m)
    return pl.pallas_call(
        _mm_body, grid=(m // tm,),
        in_specs=[_row_spec(tm, k), _full_spec((k, n))],
        out_specs=_row_spec(tm, n),
        out_shape=jax.ShapeDtypeStruct((m, n), out_dtype),
        compiler_params=_params("parallel"), name="row_matmul",
    )(x, w)


def _log_sigmoid(x):
    return jnp.minimum(x, 0.0) - jnp.log1p(jnp.exp(-jnp.abs(x)))


def _in_ab_body(x_ref, g_ref, wa_ref, wb_ref, wg_ref, gw_ref, gb_ref, za_ref, zb_ref):
    hn = _rms(x_ref[...], g_ref[...]).astype(BF16)
    za_ref[...] = jnp.dot(hn, wa_ref[...], preferred_element_type=F32).astype(za_ref.dtype)
    nb = wb_ref.shape[1]
    zb_ref[:, :nb] = jnp.dot(hn, wb_ref[...], preferred_element_type=F32)
    lowrank = jnp.dot(hn, wg_ref[...], preferred_element_type=F32)
    hi = lowrank.astype(BF16)
    lo = (lowrank - hi.astype(F32)).astype(BF16)
    pre = jnp.dot(jnp.concatenate([hi, hi, lo], axis=1), gw_ref[...], preferred_element_type=F32) + gb_ref[...]
    zb_ref[:, nb:] = _log_sigmoid(pre) / GLA_TAU


def in_proj_ab(x, gain, w_in, gwf, gbf, gwb, gbb):
    t = x.shape[0]
    tm = ROW_TILE
    na_w = 3 * NA_WIDTH
    gl_w = 2 * GLA_KW + 2 * GLA_VW
    o = np.cumsum([0, NA_WIDTH, NA_WIDTH, NA_WIDTH, GLA_KW, GLA_KW, GLA_VW, GLA_RANK, GLA_RANK, GLA_VW])
    wa = w_in[:, :o[3]].astype(BF16)
    wb = jnp.concatenate([w_in[:, o[3]:o[6]], w_in[:, o[8]:o[9]]], axis=1).astype(BF16)
    wg = w_in[:, o[6]:o[8]].astype(BF16)
    zero = jnp.zeros_like(gwf)
    gw = jnp.concatenate([jnp.concatenate([gwf, zero], axis=1), jnp.concatenate([zero, gwb], axis=1)], axis=0)
    gw_hi = gw.astype(BF16)
    gw_lo = (gw - gw_hi.astype(F32)).astype(BF16)
    gw3 = jnp.concatenate([gw_hi, gw_lo, gw_hi], axis=0)
    gbias = jnp.concatenate([gbf, gbb]).reshape(1, -1)
    return pl.pallas_call(
        _in_ab_body, grid=(t // tm,),
        in_specs=[_row_spec(tm, D_MODEL), _full_spec((1, D_MODEL)), _full_spec((D_MODEL, na_w)),
                  _full_spec((D_MODEL, gl_w)), _full_spec((D_MODEL, 2 * GLA_RANK)),
                  _full_spec((6 * GLA_RANK, 2 * GLA_KW)), _full_spec((1, 2 * GLA_KW))],
        out_specs=[_row_spec(tm, na_w), _row_spec(tm, gl_w + 2 * GLA_KW)],
        out_shape=[jax.ShapeDtypeStruct((t, na_w), BF16),
                   jax.ShapeDtypeStruct((t, gl_w + 2 * GLA_KW), F32)],
        compiler_params=_params("parallel"), name="in_proj_ab",
    )(x, gain.reshape(1, -1), wa, wb, wg, gw3, gbias)


def na_bias_table(rpb):
    rb = NA_ROWS_PER_BLOCK
    a = np.arange(rb)[:, None, None, None]
    c = np.arange(GRID_W)[None, :, None, None]
    u = np.arange(3 * rb)[None, None, :, None]
    kc = np.arange(GRID_W)[None, None, None, :]
    wstart = np.clip(c - NA_WIN_C // 2, 0, GRID_W - NA_WIN_C)
    col_ok = (kc >= wstart) & (kc < wstart + NA_WIN_C)
    dc = np.clip(kc - c + NA_WIN_C - 1, 0, 2 * NA_WIN_C - 2)
    dr = np.clip(u - a + NA_WIN_R // 2 - 1, 0, 2 * NA_WIN_R - 2)
    row_ok = [
        (u >= rb) & (u < rb + NA_WIN_R) & (a >= 0),
        (u - a >= 0) & (u - a < NA_WIN_R),
        (u >= 0) & (u < NA_WIN_R) & (a >= 0),
    ]
    shape = (rb, GRID_W, 3 * rb, GRID_W)
    flat = (NA_BLOCK_TOKENS, 3 * NA_BLOCK_TOKENS)
    pick_r = jnp.asarray(np.eye(2 * NA_WIN_R - 1, dtype=np.float32)[dr[:, 0, :, 0]])
    pick_c = jnp.asarray(np.eye(2 * NA_WIN_C - 1, dtype=np.float32)[dc[0, :, 0, :]])
    vals = jnp.einsum('aud,hdp,ckp->hacuk', pick_r, rpb.astype(F32), pick_c,
                      precision=lax.Precision.HIGHEST).reshape((NA_HEADS,) + flat)
    out = []
    for ok in row_ok:
        m = np.broadcast_to(ok & col_ok, shape).reshape(flat)
        out.append(jnp.where(jnp.asarray(m)[None], vals, NEG_INF))
    return jnp.stack(out)


def _na_body(q_ref, kp_ref, kc_ref, kn_ref, vp_ref, vc_ref, vn_ref, bias_ref, o_ref):
    lane = lax.broadcasted_iota(jnp.int32, (1, 2 * NA_HEAD_DIM), 1)
    scale = NA_HEAD_DIM ** -0.5
    nt = (((1,), (1,)), ((), ()))
    for hp in range(NA_HEADS // 2):
        cols = slice(hp * 2 * NA_HEAD_DIM, (hp + 1) * 2 * NA_HEAD_DIM)
        q = q_ref[:, cols] * scale
        ks = [r[:, cols] for r in (kp_ref, kc_ref, kn_ref)]
        vs = [r[:, cols] for r in (vp_ref, vc_ref, vn_ref)]
        o_pair = None
        for sub in range(2):
            sel = (lane < NA_HEAD_DIM) if sub == 0 else (lane >= NA_HEAD_DIM)
            qm = jnp.where(sel, q, jnp.zeros_like(q))
            s = jnp.concatenate(
                [lax.dot_general(qm, kk, nt, preferred_element_type=F32) for kk in ks], axis=1)
            s = s + bias_ref[0, 2 * hp + sub]
            p = jnp.exp(s - jnp.max(s, axis=-1, keepdims=True))
            l = jnp.sum(p, axis=-1, keepdims=True)
            pb = p.astype(BF16)
            o = None
            for j, vv in enumerate(vs):
                t = jnp.dot(pb[:, j * NA_BLOCK_TOKENS:(j + 1) * NA_BLOCK_TOKENS], vv,
                            preferred_element_type=F32)
                o = t if o is None else o + t
            o = o / l
            o_pair = o if o_pair is None else jnp.where(sel, o, o_pair)
        o_ref[:, cols] = o_pair.astype(o_ref.dtype)


def neighbourhood_attention(za, rpb, batch, n):
    t = za.shape[0]
    bt = NA_BLOCK_TOKENS
    nblk = n // bt
    assert n % bt == 0 and nblk >= 3
    bias = na_bias_table(rpb)

    def qmap(b, i):
        return (b * nblk + i, 0)

    def kmap(d, col):
        return lambda b, i: (b * nblk + jnp.clip(i + d, 0, nblk - 1), col)

    def bmap(b, i):
        return (jnp.where(i == 0, 0, jnp.where(i == nblk - 1, 2, 1)), 0, 0, 0)

    def blk(m):
        return pl.BlockSpec((bt, NA_WIDTH), m)

    return pl.pallas_call(
        _na_body, grid=(batch, nblk),
        in_specs=[blk(qmap), blk(kmap(-1, 1)), blk(kmap(0, 1)), blk(kmap(1, 1)),
                  blk(kmap(-1, 2)), blk(kmap(0, 2)), blk(kmap(1, 2)),
                  pl.BlockSpec((1, NA_HEADS, bt, 3 * bt), bmap)],
        out_specs=blk(qmap),
        out_shape=jax.ShapeDtypeStruct((t, NA_WIDTH), BF16),
        compiler_params=_params("parallel", "parallel"), name="neighbourhood_attention",
    )(za, za, za, za, za, za, za, bias)


def _split_dot(a_bf, x, dims):
    hi = x.astype(BF16)
    lo = (x - hi.astype(F32)).astype(BF16)
    return (lax.dot_general(a_bf, hi, dims, preferred_element_type=F32)
            + lax.dot_general(a_bf, lo, dims, preferred_element_type=F32))


def _gla_body(qf_ref, kf_ref, vf_ref, gf_ref, qb_ref, kb_ref, vb_ref, gb_ref,
              of_ref, ob_ref, sf_ref, sb_ref):
    c = GLA_CHUNK
    blk = GLA_BLOCK
    grp = GLA_GROUP
    nchunk = blk // c

    @pl.when(pl.program_id(1) == 0)
    def _():
        sf_ref[...] = jnp.zeros_like(sf_ref)
        sb_ref[...] = jnp.zeros_like(sb_ref)

    row = lax.broadcasted_iota(jnp.int32, (blk, blk), 0)
    col = lax.broadcasted_iota(jnp.int32, (blk, blk), 1)
    same_chunk = (row // c) == (col // c)
    grow = lax.broadcasted_iota(jnp.int32, (grp, grp), 0)
    gcol = lax.broadcasted_iota(jnp.int32, (grp, grp), 1)
    same_chunk_g = (grow // c) == (gcol // c)
    mm = (((1,), (0,)), ((), ()))
    nt = (((1,), (1,)), ((), ()))
    tn = (((0,), (0,)), ((), ()))

    def prepare(q_ref, k_ref, v_ref, g_ref, reverse):
        keep_blk = same_chunk & ((row <= col) if reverse else (row >= col))
        keep_grp = same_chunk_g & ((grow <= gcol) if reverse else (grow >= gcol))
        g = g_ref[...]
        cum = _split_dot(keep_blk.astype(BF16), g, mm)
        edge = 0 if reverse else c - 1
        tot = cum.reshape(nchunk, c, GLA_KW)[:, edge:edge + 1, :]
        tot_rows = jnp.broadcast_to(tot, (nchunk, c, GLA_KW)).reshape(blk, GLA_KW)
        q = q_ref[...] * (GLA_DK ** -0.5)
        k = k_ref[...]
        q_t = (q * jnp.exp(cum)).astype(BF16)
        k_t = (k * jnp.exp(-cum)).astype(BF16)
        k_d = (k * jnp.exp(tot_rows - cum)).astype(BF16)
        v = v_ref[...].astype(BF16)
        intra = []
        for gi in range(blk // grp):
            rows = slice(gi * grp, (gi + 1) * grp)
            per_head = []
            for h in range(GLA_HEADS):
                kc = slice(h * GLA_DK, (h + 1) * GLA_DK)
                vc = slice(h * GLA_DV, (h + 1) * GLA_DV)
                a = lax.dot_general(q_t[rows, kc], k_t[rows, kc], nt, preferred_element_type=F32)
                a = jnp.where(keep_grp, a, 0.0).astype(BF16)
                per_head.append(jnp.dot(a, v[rows, vc], preferred_element_type=F32))
            intra.append(jnp.concatenate(per_head, axis=1))
        return jnp.concatenate(intra, axis=0), q_t, k_d, v, jnp.exp(tot)

    dirs = [(prepare(qf_ref, kf_ref, vf_ref, gf_ref, False), of_ref, sf_ref, False),
            (prepare(qb_ref, kb_ref, vb_ref, gb_ref, True), ob_ref, sb_ref, True)]
    states = [[s_ref[h] for h in range(GLA_HEADS)] for _, _, s_ref, _ in dirs]
    for step in range(nchunk):
        for d, ((intra, q_t, k_d, v, decay), o_ref, _, reverse) in enumerate(dirs):
            j = nchunk - 1 - step if reverse else step
            rows = slice(j * c, (j + 1) * c)
            outs = []
            for h in range(GLA_HEADS):
                kc = slice(h * GLA_DK, (h + 1) * GLA_DK)
                vc = slice(h * GLA_DV, (h + 1) * GLA_DV)
                st = states[d][h]
                outs.append(lax.dot_general(q_t[rows, kc], st.astype(BF16), nt, preferred_element_type=F32))
                ds = lax.dot_general(v[rows, vc], k_d[rows, kc], tn, preferred_element_type=F32)
                states[d][h] = st * decay[j, :, kc] + ds
            o_ref[rows, :] = intra[rows, :] + jnp.concatenate(outs, axis=1)
    for d, (_, _, s_ref, _) in enumerate(dirs):
        for h in range(GLA_HEADS):
            s_ref[h] = states[d][h]


def gla_bidirectional(zb, batch, n):
    t = zb.shape[0]
    nb = n // GLA_BLOCK
    assert n % GLA_BLOCK == 0

    def fwd(col):
        return lambda b, i: (b * nb + i, col)

    def bwd(col):
        return lambda b, i: (b * nb + nb - 1 - i, col)

    def kw(m):
        return pl.BlockSpec((GLA_BLOCK, GLA_KW), m)

    def vw(m):
        return pl.BlockSpec((GLA_BLOCK, GLA_VW), m)

    return pl.pallas_call(
        _gla_body, grid=(batch, nb),
        in_specs=[kw(fwd(0)), kw(fwd(1)), vw(fwd(1)), kw(fwd(6)),
                  kw(bwd(0)), kw(bwd(1)), vw(bwd(1)), kw(bwd(7))],
        out_specs=[vw(fwd(0)), vw(bwd(0))],
        out_shape=[jax.ShapeDtypeStruct((t, GLA_VW), F32)] * 2,
        scratch_shapes=[pltpu.VMEM((GLA_HEADS, GLA_DV, GLA_DK), F32)] * 2,
        compiler_params=_params("parallel", "arbitrary"), name="gla_bidirectional",
    )(zb, zb, zb, zb, zb, zb, zb, zb)


def _silu(x):
    return x * jax.nn.sigmoid(x)


def _out_ab_body(oa_ref, of_ref, ob_ref, r_ref, gn_ref, wa_ref, wb_ref, x_ref, o_ref):
    o = of_ref[...] + ob_ref[...]
    r = r_ref[...]
    parts = []
    for h in range(GLA_HEADS):
        vc = slice(h * GLA_DV, (h + 1) * GLA_DV)
        parts.append((_rms(o[:, vc], gn_ref[...]) * _silu(r[:, vc])).astype(BF16))
    o_b = jnp.concatenate(parts, axis=1)
    o_ref[...] = (jnp.dot(oa_ref[...], wa_ref[...], preferred_element_type=F32)
                  + jnp.dot(o_b, wb_ref[...], preferred_element_type=F32) + x_ref[...])


def out_proj_ab(o_a, o_fw, o_bw, zb, gla_norm, w_out, x):
    t = x.shape[0]
    tm = ROW_TILE
    w = w_out.astype(BF16)
    return pl.pallas_call(
        _out_ab_body, grid=(t // tm,),
        in_specs=[_row_spec(tm, NA_WIDTH), _row_spec(tm, GLA_VW), _row_spec(tm, GLA_VW),
                  _row_spec(tm, GLA_VW, col=2), _full_spec((1, GLA_DV)),
                  _full_spec((NA_WIDTH, D_MODEL)), _full_spec((GLA_VW, D_MODEL)),
                  _row_spec(tm, D_MODEL)],
        out_specs=_row_spec(tm, D_MODEL),
        out_shape=jax.ShapeDtypeStruct((t, D_MODEL), F32),
        compiler_params=_params("parallel"), name="out_proj_ab",
    )(o_a, o_fw, o_bw, zb, gla_norm.reshape(1, -1), w[:NA_WIDTH], w[NA_WIDTH:], x)


def _in_c_body(x_ref, g_ref, wbg_ref, wcg_ref, wxt_ref, bg_ref, u_ref):
    hn = _rms(x_ref[...], g_ref[...]).astype(BF16)
    bg_ref[...] = jnp.dot(hn, wbg_ref[...], preferred_element_type=F32)
    u_ref[...] = (jnp.dot(hn, wcg_ref[...], preferred_element_type=F32)
                  * jnp.dot(hn, wxt_ref[...], preferred_element_type=F32))


def in_proj_c(x, gain, w_in):
    t = x.shape[0]
    tm = ROW_TILE
    w = w_in.astype(BF16)
    d = D_MODEL
    return pl.pallas_call(
        _in_c_body, grid=(t // tm,),
        in_specs=[_row_spec(tm, d), _full_spec((1, d))] + [_full_spec((d, d))] * 3,
        out_specs=[_row_spec(tm, d)] * 2,
        out_shape=[jax.ShapeDtypeStruct((t, d), F32)] * 2,
        compiler_params=_params("parallel"), name="in_proj_c",
    )(x, gain.reshape(1, -1), w[:, :d], w[:, d:2 * d], w[:, 2 * d:])


def _conv_out_body(u_ref, up_ref, un_ref, bg_ref, cw_ref, w_ref, x_ref, o_ref, *, tiles_per_seq):
    i = pl.program_id(0)
    tm = u_ref.shape[0]
    u = u_ref[...]
    row = lax.broadcasted_iota(jnp.int32, (tm, 1), 0)
    first = (i % tiles_per_seq) == 0
    last = (i % tiles_per_seq) == tiles_per_seq - 1
    prev_row = jnp.where(first, 0.0, up_ref[SUBLANES - 1:SUBLANES, :])
    next_row = jnp.where(last, 0.0, un_ref[0:1, :])
    u_prev = jnp.where(row == 0, prev_row, pltpu.roll(u, 1, axis=0))
    u_next = jnp.where(row == tm - 1, next_row, pltpu.roll(u, tm - 1, axis=0))
    conv = cw_ref[0:1, :] * u_prev + cw_ref[1:2, :] * u + cw_ref[2:3, :] * u_next
    o_ref[...] = (jnp.dot((bg_ref[...] * conv).astype(BF16), w_ref[...], preferred_element_type=F32)
                  + x_ref[...])


def conv_out_proj(u, bg, conv_w, w_out, x, n):
    t = x.shape[0]
    tm = ROW_TILE
    d = D_MODEL
    per = tm // SUBLANES
    nhalo = t // SUBLANES
    assert n % tm == 0
    return pl.pallas_call(
        functools.partial(_conv_out_body, tiles_per_seq=n // tm), grid=(t // tm,),
        in_specs=[_row_spec(tm, d),
                  pl.BlockSpec((SUBLANES, d), lambda i: (jnp.maximum(i * per - 1, 0), 0)),
                  pl.BlockSpec((SUBLANES, d), lambda i: (jnp.minimum((i + 1) * per, nhalo - 1), 0)),
                  _row_spec(tm, d), _full_spec((CONV_W, d)), _full_spec((d, d)), _row_spec(tm, d)],
        out_specs=_row_spec(tm, d),
        out_shape=jax.ShapeDtypeStruct((t, d), F32),
        compiler_params=_params("parallel"), name="conv_out_proj",
    )(u, u, u, bg, conv_w, w_out.astype(BF16), x)


def _mem_body(x_ref, g_ref, wq_ref, k_ref, v_ref, wo_ref, o_ref):
    x = x_ref[...]
    hn = _rms(x, g_ref[...]).astype(BF16)
    q = (jnp.dot(hn, wq_ref[...], preferred_element_type=F32) * MEM_HEAD_DIM ** -0.5).astype(BF16)
    nt = (((1,), (1,)), ((), ()))
    parts = []
    for h in range(MEM_HEADS):
        hc = slice(h * MEM_HEAD_DIM, (h + 1) * MEM_HEAD_DIM)
        s = lax.dot_general(q[:, hc], k_ref[0, :, hc], nt, preferred_element_type=F32)
        p = jnp.exp(s - jnp.max(s, axis=-1, keepdims=True))
        l = jnp.sum(p, axis=-1, keepdims=True)
        o = jnp.dot(p.astype(BF16), v_ref[0, :, hc], preferred_element_type=F32) / l
        parts.append(o.astype(BF16))
    o_ref[...] = jnp.dot(jnp.concatenate(parts, axis=1), wo_ref[...], preferred_element_type=F32) + x


def memory_attention(x, mem, gain, wq, wk, wv, wo, n):
    t, d = x.shape
    b, nm, _ = mem.shape
    tm = ROW_TILE
    per = n // tm
    memf = mem.reshape(b * nm, d)
    k = matmul(memf, wk.astype(BF16), BF16).reshape(b, nm, d)
    v = matmul(memf, wv.astype(BF16), BF16).reshape(b, nm, d)
    kv_spec = pl.BlockSpec((1, nm, d), lambda i: (i // per, 0, 0))
    return pl.pallas_call(
        _mem_body, grid=(t // tm,),
        in_specs=[_row_spec(tm, d), _full_spec((1, d)), _full_spec((d, d)), kv_spec, kv_spec,
                  _full_spec((d, d))],
        out_specs=_row_spec(tm, d),
        out_shape=jax.ShapeDtypeStruct((t, d), F32),
        compiler_params=_params("parallel"), name="memory_attention",
    )(x, gain.reshape(1, -1), wq.astype(BF16), k, v, wo.astype(BF16))


def _router_body(x_ref, g_ref, r_ref, aff_ref):
    hn = _rms(x_ref[...], g_ref[...])
    r = r_ref[...]
    hn_hi, r_hi = hn.astype(BF16), r.astype(BF16)
    hn_lo, r_lo = (hn - hn_hi.astype(F32)).astype(BF16), (r - r_hi.astype(F32)).astype(BF16)
    nt = (((1,), (1,)), ((), ()))
    logits = (lax.dot_general(r_hi, hn_hi, nt, preferred_element_type=F32)
              + lax.dot_general(r_hi, hn_lo, nt, preferred_element_type=F32)
              + lax.dot_general(r_lo, hn_hi, nt, preferred_element_type=F32))
    p = jnp.exp(logits - jnp.max(logits, axis=0, keepdims=True))
    aff_ref[...] = p / jnp.sum(p, axis=0, keepdims=True)


def router_affinity(x, gain, router):
    t, d = x.shape
    tm = ROW_TILE
    return pl.pallas_call(
        _router_body, grid=(t // tm,),
        in_specs=[_row_spec(tm, d), _full_spec((1, d)), _full_spec((N_EXPERTS, d))],
        out_specs=pl.BlockSpec((N_EXPERTS, tm), lambda i: (0, i)),
        out_shape=jax.ShapeDtypeStruct((N_EXPERTS, t), F32),
        compiler_params=_params("parallel"), name="router_affinity",
    )(x, gain.reshape(1, -1), router.T)


def _select_body(aff_ref, slot_ref, base_ref, cnt_ref, *, cap):
    e, nc, _ = aff_ref.shape
    per = MOE_TILE // LANES
    bits = pltpu.bitcast(aff_ref[...], jnp.int32)

    def count(mask):
        s = jnp.sum(mask.astype(jnp.int32), axis=1, keepdims=True)
        return jnp.sum(s, axis=2, keepdims=True)

    def bisect(b, thr):
        cand = thr | (jnp.int32(1) << (30 - b))
        return jnp.where(count(bits >= cand) >= cap, cand, thr)

    thr = lax.fori_loop(0, 31, bisect, jnp.zeros((e, 1, 1), jnp.int32))
    gt = bits > thr
    eq = bits == thr
    need_eq = cap - count(gt)

    li = lax.broadcasted_iota(jnp.int32, (LANES, LANES), 0)
    lj = lax.broadcasted_iota(jnp.int32, (LANES, LANES), 1)
    incl = (li <= lj).astype(BF16)
    ci = lax.broadcasted_iota(jnp.int32, (nc, nc), 0)
    cj = lax.broadcasted_iota(jnp.int32, (nc, nc), 1)
    strict = (cj < ci).astype(BF16)
    same_tile = (cj // per == ci // per).astype(BF16)
    same_tile_before = ((cj // per == ci // per) & (cj < ci)).astype(BF16)
    tiles_before = ((cj // per < ci // per) & (cj % per == 0)).astype(BF16)

    def chunk_prefix(m):
        within = jnp.dot(m, incl, preferred_element_type=F32)
        tot = jnp.broadcast_to(within[:, LANES - 1:LANES], (nc, LANES)).astype(BF16)
        return within, tot

    for x in range(e):
        eq_x = eq[x].astype(BF16)
        within, tot = chunk_prefix(eq_x)
        rank_eq = within - eq_x.astype(F32) + jnp.dot(strict, tot, preferred_element_type=F32)
        sel = gt[x] | (eq[x] & (rank_eq < need_eq[x].astype(F32)))
        sel_b = sel.astype(BF16)
        within, tot = chunk_prefix(sel_b)
        tile_cnt = jnp.dot(same_tile, tot, preferred_element_type=F32).astype(jnp.int32)
        base = jnp.dot(tiles_before, tile_cnt.astype(BF16), preferred_element_type=F32)
        in_tile = jnp.dot(same_tile_before, tot, preferred_element_type=F32)
        pos = (base + in_tile + within - sel_b.astype(F32)).astype(jnp.int32)
        slot_ref[x] = jnp.where(sel, pos, NOT_SELECTED)
        base_ref[x] = base.astype(jnp.int32)
        cnt_ref[x] = tile_cnt


def select_slots(aff_t, cap):
    e, t = aff_t.shape
    nc = t // LANES
    per = MOE_TILE // LANES
    slot, base, cnt = pl.pallas_call(
        functools.partial(_select_body, cap=cap),
        out_shape=[jax.ShapeDtypeStruct((e, nc, LANES), jnp.int32)] * 3,
        compiler_params=pltpu.CompilerParams(vmem_limit_bytes=VMEM_LIMIT_BYTES), name="select_slots",
    )(aff_t.reshape(e, nc, LANES))
    return slot.reshape(e, t), base[:, ::per, 0].reshape(-1), cnt[:, ::per, 0].reshape(-1)


def _one_hot_t(slot_ref, lo, start, width):
    tm = slot_ref.shape[1]
    w = lax.broadcasted_iota(jnp.int32, (width, tm), 0)
    rows = []
    for e in range(N_EXPERTS):
        s = slot_ref[e:e + 1, :]
        rows.append((s - start[e] == w) & (s >= lo[e]) & (s < lo[e] + width))
    return jnp.concatenate(rows, axis=0)


def _tile_scalars(base_ref, cnt_ref, i, nt):
    lo = [base_ref[e * nt + i] for e in range(N_EXPERTS)]
    hi = [lo[e] + cnt_ref[e * nt + i] for e in range(N_EXPERTS)]
    start = [(l // SLAB_ALIGN) * SLAB_ALIGN for l in lo]
    return start, hi


def _rounds(start, hi):
    m = hi[0] - start[0]
    for e in range(1, N_EXPERTS):
        m = jnp.maximum(m, hi[e] - start[e])
    return (m + MOE_SLAB - 1) // MOE_SLAB


def _dispatch_body(base_ref, cnt_ref, x_ref, g_ref, slot_ref, aff_ref, xe_ref, ge_ref,
                   xbuf, gbuf, xover, gover, xcarry, gcarry, sem, osem, *, nt, cap):
    i = pl.program_id(0)
    cur = i % 2
    w_ = MOE_SLAB
    al = SLAB_ALIGN
    hn = _rms(x_ref[...], g_ref[...]).astype(BF16)
    start, hi = _tile_scalars(base_ref, cnt_ref, i, nt)
    nxt = [(h // al) * al for h in hi]

    @pl.when(i == 0)
    def _():
        xcarry[...] = jnp.zeros_like(xcarry)
        gcarry[...] = jnp.zeros_like(gcarry)

    def gather(hot):
        rows = jnp.dot(hot.astype(BF16), hn, preferred_element_type=F32)
        hot_f = hot.astype(F32)
        n = hot.shape[0] // N_EXPERTS
        gates = [jnp.sum(hot_f[e * n:(e + 1) * n] * aff_ref[e:e + 1, :], axis=1, keepdims=True)
                 for e in range(N_EXPERTS)]
        return rows, gates

    def copies(xsrc, gsrc, st, e, xs, gs):
        dst = pl.ds(pl.multiple_of(st[e], al), w_)
        return (pltpu.make_async_copy(xsrc.at[e], xe_ref.at[e, dst], xs),
                pltpu.make_async_copy(gsrc.at[e], ge_ref.at[e, dst], gs))

    rows, gates = gather(_one_hot_t(slot_ref, start, start, w_))
    keep = []
    for e in range(N_EXPERTS):
        xc = xcarry[e].astype(F32)
        gc = gcarry[e]
        win = rows[e * w_:(e + 1) * w_]
        xbuf[cur, e] = jnp.concatenate([win[:al] + xc, win[al:]], axis=0).astype(BF16)
        gwin = jnp.broadcast_to(gates[e], (w_, LANES))
        gbuf[cur, e] = jnp.concatenate([gwin[:al] + gc, gwin[al:]], axis=0)
        keep.append((xc, gc))
    rows, gates = gather(_one_hot_t(slot_ref, nxt, nxt, al))
    for e in range(N_EXPERTS):
        same = (nxt[e] == start[e]).astype(F32)
        xcarry[e] = (rows[e * al:(e + 1) * al] + same * keep[e][0]).astype(BF16)
        gcarry[e] = jnp.broadcast_to(gates[e], (al, LANES)) + same * keep[e][1]

    @pl.when(i > 0)
    def _():
        st_p, _ = _tile_scalars(base_ref, cnt_ref, i - 1, nt)
        for e in range(N_EXPERTS):
            for cp in copies(xbuf.at[1 - cur], gbuf.at[1 - cur], st_p, e, sem.at[1 - cur, 0], sem.at[1 - cur, 1]):
                cp.wait()

    for e in range(N_EXPERTS):
        for cp in copies(xbuf.at[cur], gbuf.at[cur], start, e, sem.at[cur, 0], sem.at[cur, 1]):
            cp.start()

    def extra_round(k, carry):
        st_k = [s + k * w_ for s in start]
        rows, gates = gather(_one_hot_t(slot_ref, st_k, st_k, w_))
        for e in range(N_EXPERTS):
            xover[e] = rows[e * w_:(e + 1) * w_].astype(BF16)
            gover[e] = jnp.broadcast_to(gates[e], (w_, LANES))
        for e in range(N_EXPERTS):
            @pl.when(hi[e] - start[e] > k * w_)
            def _():
                for cp in copies(xover, gover, st_k, e, osem.at[0], osem.at[1]):
                    cp.start()
        for e in range(N_EXPERTS):
            @pl.when(hi[e] - start[e] > k * w_)
            def _():
                for cp in copies(xover, gover, st_k, e, osem.at[0], osem.at[1]):
                    cp.wait()
        return carry

    lax.fori_loop(1, _rounds(start, hi), extra_round, 0)

    @pl.when(i == nt - 1)
    def _():
        for e in range(N_EXPERTS):
            for cp in copies(xbuf.at[cur], gbuf.at[cur], start, e, sem.at[cur, 0], sem.at[cur, 1]):
                cp.wait()
        for e in range(N_EXPERTS):
            xover[e] = jnp.zeros((w_, xover.shape[-1]), BF16)
            gover[e] = jnp.zeros((w_, LANES), F32)
        tail = [cap] * N_EXPERTS
        for e in range(N_EXPERTS):
            for cp in copies(xover, gover, tail, e, osem.at[0], osem.at[1]):
                cp.start()
        for e in range(N_EXPERTS):
            for cp in copies(xover, gover, tail, e, osem.at[0], osem.at[1]):
                cp.wait()


def dispatch(x, gain, slot, aff_t, base, cnt, cap):
    t, d = x.shape
    tm = MOE_TILE
    nt = t // tm
    rows = cap + MOE_SLAB
    any_spec = pl.BlockSpec(memory_space=pl.ANY)
    grid_spec = pltpu.PrefetchScalarGridSpec(
        num_scalar_prefetch=2, grid=(nt,),
        in_specs=[pl.BlockSpec((tm, d), lambda i, *_: (i, 0)), pl.BlockSpec((1, d), lambda i, *_: (0, 0)),
                  pl.BlockSpec((N_EXPERTS, tm), lambda i, *_: (0, i)),
                  pl.BlockSpec((N_EXPERTS, tm), lambda i, *_: (0, i))],
        out_specs=[any_spec, any_spec],
        scratch_shapes=[pltpu.VMEM((2, N_EXPERTS, MOE_SLAB, d), BF16),
                        pltpu.VMEM((2, N_EXPERTS, MOE_SLAB, LANES), F32),
                        pltpu.VMEM((N_EXPERTS, MOE_SLAB, d), BF16),
                        pltpu.VMEM((N_EXPERTS, MOE_SLAB, LANES), F32),
                        pltpu.VMEM((N_EXPERTS, SLAB_ALIGN, d), BF16),
                        pltpu.VMEM((N_EXPERTS, SLAB_ALIGN, LANES), F32),
                        pltpu.SemaphoreType.DMA((2, 2)), pltpu.SemaphoreType.DMA((2,))])
    return pl.pallas_call(
        functools.partial(_dispatch_body, nt=nt, cap=cap), grid_spec=grid_spec,
        out_shape=[jax.ShapeDtypeStruct((N_EXPERTS, rows, d), BF16),
                   jax.ShapeDtypeStruct((N_EXPERTS, rows, LANES), F32)],
        compiler_params=_params("arbitrary"), name="moe_dispatch",
    )(base, cnt, x, gain.reshape(1, -1), slot, aff_t)


def _ffn_body(xe_ref, gate_ref, wg_ref, wu_ref, wd_ref, o_ref):
    xe = xe_ref[0]
    a = jnp.dot(xe, wg_ref[0], preferred_element_type=F32)
    b = jnp.dot(xe, wu_ref[0], preferred_element_type=F32)
    hid = (_silu(a) * b).astype(BF16)
    o_ref[0] = (jnp.dot(hid, wd_ref[0], preferred_element_type=F32) * gate_ref[0][:, :1]).astype(o_ref.dtype)


def expert_ffn(xe, gate, wg, wu, wd, cap):
    e, _, d = xe.shape
    f = wg.shape[-1]
    tm = FFN_TILE
    return pl.pallas_call(
        _ffn_body, grid=(e, cap // tm),
        in_specs=[
            pl.BlockSpec((1, tm, d), lambda i, j: (i, j, 0)),
            pl.BlockSpec((1, tm, LANES), lambda i, j: (i, j, 0)),
            pl.BlockSpec((1, d, f), lambda i, j: (i, 0, 0)),
            pl.BlockSpec((1, d, f), lambda i, j: (i, 0, 0)),
            pl.BlockSpec((1, f, d), lambda i, j: (i, 0, 0)),
        ],
        out_specs=pl.BlockSpec((1, tm, d), lambda i, j: (i, j, 0)),
        out_shape=jax.ShapeDtypeStruct((e, cap, d), BF16),
        compiler_params=_params("parallel", "parallel"), name="expert_ffn",
    )(xe, gate, wg, wu, wd)


def _combine_body(base_ref, cnt_ref, x_ref, slot_ref, y_ref, o_ref, ybuf, yover, sem, osem, *, nt, cap):
    i = pl.program_id(0)
    cur = i % 2
    w_ = MOE_SLAB
    tn = (((0,), (0,)), ((), ()))

    def window(st):
        return [pl.multiple_of(jnp.minimum(s, cap - w_), SLAB_ALIGN) for s in st]

    def fetch(step, buf, xs):
        st, _ = _tile_scalars(base_ref, cnt_ref, step, nt)
        ws = window(st)
        return [pltpu.make_async_copy(y_ref.at[e, pl.ds(ws[e], w_)], buf.at[e], xs)
                for e in range(N_EXPERTS)]

    @pl.when(i == 0)
    def _():
        yover[...] = jnp.zeros_like(yover)
        for cp in fetch(0, ybuf.at[0], sem.at[0]):
            cp.start()

    @pl.when(i + 1 < nt)
    def _():
        for cp in fetch(i + 1, ybuf.at[1 - cur], sem.at[1 - cur]):
            cp.start()

    for cp in fetch(i, ybuf.at[cur], sem.at[cur]):
        cp.wait()

    start, hi = _tile_scalars(base_ref, cnt_ref, i, nt)

    def scatter(buf, st_k):
        hot = _one_hot_t(slot_ref, st_k, window(st_k), w_).astype(BF16)
        ys = buf[...].reshape(N_EXPERTS * w_, buf.shape[-1])
        return lax.dot_general(hot, ys, tn, preferred_element_type=F32)

    o_ref[...] = x_ref[...] + scatter(ybuf.at[cur], start)

    def extra_round(k, carry):
        st_k = [s + k * w_ for s in start]
        ws = window(st_k)
        for e in range(N_EXPERTS):
            @pl.when(hi[e] - start[e] > k * w_)
            def _():
                pltpu.make_async_copy(y_ref.at[e, pl.ds(ws[e], w_)], yover.at[e], osem).start()
        for e in range(N_EXPERTS):
            @pl.when(hi[e] - start[e] > k * w_)
            def _():
                pltpu.make_async_copy(y_ref.at[e, pl.ds(ws[e], w_)], yover.at[e], osem).wait()
        o_ref[...] += scatter(yover, st_k)
        return carry

    lax.fori_loop(1, _rounds(start, hi), extra_round, 0)


def combine(x, slot, y, base, cnt, cap):
    t, d = x.shape
    tm = MOE_TILE
    nt = t // tm
    grid_spec = pltpu.PrefetchScalarGridSpec(
        num_scalar_prefetch=2, grid=(nt,),
        in_specs=[pl.BlockSpec((tm, d), lambda i, *_: (i, 0)),
                  pl.BlockSpec((N_EXPERTS, tm), lambda i, *_: (0, i)),
                  pl.BlockSpec(memory_space=pl.ANY)],
        out_specs=pl.BlockSpec((tm, d), lambda i, *_: (i, 0)),
        scratch_shapes=[pltpu.VMEM((2, N_EXPERTS, MOE_SLAB, d), BF16),
                        pltpu.VMEM((N_EXPERTS, MOE_SLAB, d), BF16),
                        pltpu.SemaphoreType.DMA((2,)), pltpu.SemaphoreType.DMA(())])
    return pl.pallas_call(
        functools.partial(_combine_body, nt=nt, cap=cap), grid_spec=grid_spec,
        out_shape=jax.ShapeDtypeStruct((t, d), F32),
        compiler_params=_params("arbitrary"), name="moe_combine",
    )(base, cnt, x, slot, y)


def _norm_body(x_ref, g_ref, o_ref):
    o_ref[...] = _rms(x_ref[...], g_ref[...]).astype(o_ref.dtype)


def rms_norm_rows(x, gain, out_dtype):
    t, d = x.shape
    tm = ROW_TILE
    return pl.pallas_call(
        _norm_body, grid=(t // tm,),
        in_specs=[_row_spec(tm, d), _full_spec((1, d))],
        out_specs=_row_spec(tm, d),
        out_shape=jax.ShapeDtypeStruct((t, d), out_dtype),
        compiler_params=_params("parallel"), name="rms_norm_rows",
    )(x, gain.reshape(1, -1))


def expert_choice_ffn(x, layer, p):
    t, _ = x.shape
    gain = p['norm_ffn'][layer]
    cap = EC_CAPACITY * t // N_EXPERTS
    aff_t = router_affinity(x, gain, p['router'][layer])
    slot, base, cnt = select_slots(aff_t, cap)
    xe, ge = dispatch(x, gain, slot, aff_t, base, cnt, cap)
    y = expert_ffn(xe, ge, p['w_gate'][layer], p['w_up'][layer], p['w_down'][layer], cap)
    return combine(x, slot, y, base, cnt, cap)


def trunk(x, mem, p):
    b, n, d = x.shape
    x = x.reshape(b * n, d)
    for layer in range(DEPTH):
        i = layer // 2
        if layer % 2 == 0:
            za, zb = in_proj_ab(x, p['norm_mix'][layer], p['w_in_ab'][i], p['gla_gw_f'][i],
                                p['gla_gb_f'][i], p['gla_gw_b'][i], p['gla_gb_b'][i])
            o_a = neighbourhood_attention(za, p['na_rpb'][i], b, n)
            o_fw, o_bw = gla_bidirectional(zb, b, n)
            x = out_proj_ab(o_a, o_fw, o_bw, zb, p['gla_norm'][i], p['w_out_ab'][i], x)
        else:
            bg, u = in_proj_c(x, p['norm_mix'][layer], p['w_in_c'][i])
            x = conv_out_proj(u, bg, p['conv_w'][i], p['w_out_c'][i], x, n)
        x = memory_attention(x, mem, p['norm_mem'][layer], p['w_mq'][layer], p['w_mk'][layer],
                             p['w_mv'][layer], p['w_mo'][layer], n)
        x = expert_choice_ffn(x, layer, p)
    return rms_norm_rows(x, p['final_norm'], F32).reshape(b, n, d)


def kernel(x_prompt, x_sample, mem_prompt, mem_sample, w_in_ab, na_rpb, gla_gw_f, gla_gb_f, gla_gw_b,
           gla_gb_b, gla_norm, w_out_ab, w_in_c, conv_w, w_out_c, norm_mix, norm_mem, norm_ffn,
           w_mq, w_mk, w_mv, w_mo, router, w_gate, w_up, w_down, final_norm):
    p = dict(w_in_ab=w_in_ab, na_rpb=na_rpb, gla_gw_f=gla_gw_f, gla_gb_f=gla_gb_f, gla_gw_b=gla_gw_b,
             gla_gb_b=gla_gb_b, gla_norm=gla_norm, w_out_ab=w_out_ab, w_in_c=w_in_c, conv_w=conv_w,
             w_out_c=w_out_c, norm_mix=norm_mix, norm_mem=norm_mem, norm_ffn=norm_ffn, w_mq=w_mq,
             w_mk=w_mk, w_mv=w_mv, w_mo=w_mo, router=router, w_gate=w_gate.astype(BF16),
             w_up=w_up.astype(BF16), w_down=w_down.astype(BF16), final_norm=final_norm)
    y_prompt = trunk(x_prompt, mem_prompt, p)
    y_sample = trunk(x_sample, mem_sample, p)
    return (y_prompt, y_sample)
```

```python
import functools

import jax
import jax.numpy as jnp
import numpy as np
from jax import lax
from jax.experimental import pallas as pl
from jax.experimental.pallas import tpu as pltpu

D_MODEL = 1024
DEPTH = 4
GRID_W = 64
EPS = 1e-6
NEG_INF = -1e30
NA_HEADS = 8
NA_HEAD_DIM = 64
NA_WIN_R = 8
NA_WIN_C = 16
NA_WIDTH = NA_HEADS * NA_HEAD_DIM
GLA_HEADS = 4
GLA_DK = 64
GLA_DV = 128
GLA_RANK = 16
GLA_TAU = 16.0
GLA_CHUNK = 64
GLA_KW = GLA_HEADS * GLA_DK
GLA_VW = GLA_HEADS * GLA_DV
CONV_W = 3
MEM_HEADS = 4
MEM_HEAD_DIM = D_MODEL // MEM_HEADS
N_EXPERTS = 16
D_EXPERT = 2 * D_MODEL
EC_CAPACITY = 2

VMEM_LIMIT_BYTES = 48 * 1024 * 1024
ROW_TILE = 512
SUBLANES = 8
NA_ROWS_PER_BLOCK = 4
NA_BLOCK_TOKENS = NA_ROWS_PER_BLOCK * GRID_W
GLA_BLOCK = 512
GLA_GROUP = 256
LANES = 128
NOT_SELECTED = -(1 << 24)
MOE_TILE = 256
MOE_SLAB = 64
SLAB_ALIGN = 16
FFN_TILE = 512
BF16 = jnp.bfloat16
F32 = jnp.float32


def _params(*sem):
    return pltpu.CompilerParams(dimension_semantics=sem, vmem_limit_bytes=VMEM_LIMIT_BYTES)


def _rms(x, g):
    return x * lax.rsqrt(jnp.mean(x * x, axis=-1, keepdims=True) + EPS) * g


def _row_spec(tm, n, col=0):
    return pl.BlockSpec((tm, n), lambda i: (i, col))


def _full_spec(shape):
    return pl.BlockSpec(shape, lambda *_: (0,) * len(shape))


def _mm_body(x_ref, w_ref, o_ref):
    o_ref[...] = jnp.dot(x_ref[...].astype(BF16), w_ref[...],
                         preferred_element_type=F32).astype(o_ref.dtype)


def matmul(x, w, out_dtype):
    m, k = x.shape
    n = w.shape[1]
    tm = min(ROW_TILE, m)
    return pl.pallas_call(
        _mm_body, grid=(m // tm,),
        in_specs=[_row_spec(tm, k), _full_spec((k, n))],
        out_specs=_row_spec(tm, n),
        out_shape=jax.ShapeDtypeStruct((m, n), out_dtype),
        compiler_params=_params("parallel"), name="row_matmul",
    )(x, w)


def _log_sigmoid(x):
    return jnp.minimum(x, 0.0) - jnp.log1p(jnp.exp(-jnp.abs(x)))


def _in_ab_body(x_ref, g_ref, wa_ref, wb_ref, wg_ref, gw_ref, gb_ref, za_ref, zb_ref):
    hn = _rms(x_ref[...], g_ref[...]).astype(BF16)
    za_ref[...] = jnp.dot(hn, wa_ref[...], preferred_element_type=F32).astype(za_ref.dtype)
    nb = wb_ref.shape[1]
    zb_ref[:, :nb] = jnp.dot(hn, wb_ref[...], preferred_element_type=F32)
    lowrank = jnp.dot(hn, wg_ref[...], preferred_element_type=F32)
    hi = lowrank.astype(BF16)
    lo = (lowrank - hi.astype(F32)).astype(BF16)
    pre = jnp.dot(jnp.concatenate([hi, hi, lo], axis=1), gw_ref[...], preferred_element_type=F32) + gb_ref[...]
    zb_ref[:, nb:] = _log_sigmoid(pre) / GLA_TAU


def in_proj_ab(x, gain, w_in, gwf, gbf, gwb, gbb):
    t = x.shape[0]
    tm = ROW_TILE
    na_w = 3 * NA_WIDTH
    gl_w = 2 * GLA_KW + 2 * GLA_VW
    o = np.cumsum([0, NA_WIDTH, NA_WIDTH, NA_WIDTH, GLA_KW, GLA_KW, GLA_VW, GLA_RANK, GLA_RANK, GLA_VW])
    wa = w_in[:, :o[3]].astype(BF16)
    wb = jnp.concatenate([w_in[:, o[3]:o[6]], w_in[:, o[8]:o[9]]], axis=1).astype(BF16)
    wg = w_in[:, o[6]:o[8]].astype(BF16)
    zero = jnp.zeros_like(gwf)
    gw = jnp.concatenate([jnp.concatenate([gwf, zero], axis=1), jnp.concatenate([zero, gwb], axis=1)], axis=0)
    gw_hi = gw.astype(BF16)
    gw_lo = (gw - gw_hi.astype(F32)).astype(BF16)
    gw3 = jnp.concatenate([gw_hi, gw_lo, gw_hi], axis=0)
    gbias = jnp.concatenate([gbf, gbb]).reshape(1, -1)
    return pl.pallas_call(
        _in_ab_body, grid=(t // tm,),
        in_specs=[_row_spec(tm, D_MODEL), _full_spec((1, D_MODEL)), _full_spec((D_MODEL, na_w)),
                  _full_spec((D_MODEL, gl_w)), _full_spec((D_MODEL, 2 * GLA_RANK)),
                  _full_spec((6 * GLA_RANK, 2 * GLA_KW)), _full_spec((1, 2 * GLA_KW))],
        out_specs=[_row_spec(tm, na_w), _row_spec(tm, gl_w + 2 * GLA_KW)],
        out_shape=[jax.ShapeDtypeStruct((t, na_w), BF16),
                   jax.ShapeDtypeStruct((t, gl_w + 2 * GLA_KW), F32)],
        compiler_params=_params("parallel"), name="in_proj_ab",
    )(x, gain.reshape(1, -1), wa, wb, wg, gw3, gbias)


def na_bias_table(rpb):
    rb = NA_ROWS_PER_BLOCK
    a = np.arange(rb)[:, None, None, None]
    c = np.arange(GRID_W)[None, :, None, None]
    u = np.arange(3 * rb)[None, None, :, None]
    kc = np.arange(GRID_W)[None, None, None, :]
    wstart = np.clip(c - NA_WIN_C // 2, 0, GRID_W - NA_WIN_C)
    col_ok = (kc >= wstart) & (kc < wstart + NA_WIN_C)
    dc = np.clip(kc - c + NA_WIN_C - 1, 0, 2 * NA_WIN_C - 2)
    dr = np.clip(u - a + NA_WIN_R // 2 - 1, 0, 2 * NA_WIN_R - 2)
    row_ok = [
        (u >= rb) & (u < rb + NA_WIN_R) & (a >= 0),
        (u - a >= 0) & (u - a < NA_WIN_R),
        (u >= 0) & (u < NA_WIN_R) & (a >= 0),
    ]
    shape = (rb, GRID_W, 3 * rb, GRID_W)
    flat = (NA_BLOCK_TOKENS, 3 * NA_BLOCK_TOKENS)
    pick_r = jnp.asarray(np.eye(2 * NA_WIN_R - 1, dtype=np.float32)[dr[:, 0, :, 0]])
    pick_c = jnp.asarray(np.eye(2 * NA_WIN_C - 1, dtype=np.float32)[dc[0, :, 0, :]])
    vals = jnp.einsum('aud,hdp,ckp->hacuk', pick_r, rpb.astype(F32), pick_c,
                      precision=lax.Precision.HIGHEST).reshape((NA_HEADS,) + flat)
    out = []
    for ok in row_ok:
        m = np.broadcast_to(ok & col_ok, shape).reshape(flat)
        out.append(jnp.where(jnp.asarray(m)[None], vals, NEG_INF))
    return jnp.stack(out)


def _na_body(q_ref, kp_ref, kc_ref, kn_ref, vp_ref, vc_ref, vn_ref, bias_ref, o_ref):
    lane = lax.broadcasted_iota(jnp.int32, (1, 2 * NA_HEAD_DIM), 1)
    scale = NA_HEAD_DIM ** -0.5
    nt = (((1,), (1,)), ((), ()))
    for hp in range(NA_HEADS // 2):
        cols = slice(hp * 2 * NA_HEAD_DIM, (hp + 1) * 2 * NA_HEAD_DIM)
        q = q_ref[:, cols] * scale
        ks = [r[:, cols] for r in (kp_ref, kc_ref, kn_ref)]
        vs = [r[:, cols] for r in (vp_ref, vc_ref, vn_ref)]
        o_pair = None
        for sub in range(2):
            sel = (lane < NA_HEAD_DIM) if sub == 0 else (lane >= NA_HEAD_DIM)
            qm = jnp.where(sel, q, jnp.zeros_like(q))
            s = jnp.concatenate(
                [lax.dot_general(qm, kk, nt, preferred_element_type=F32) for kk in ks], axis=1)
            s = s + bias_ref[0, 2 * hp + sub]
            p = jnp.exp(s - jnp.max(s, axis=-1, keepdims=True))
            l = jnp.sum(p, axis=-1, keepdims=True)
            pb = p.astype(BF16)
            o = None
            for j, vv in enumerate(vs):
                t = jnp.dot(pb[:, j * NA_BLOCK_TOKENS:(j + 1) * NA_BLOCK_TOKENS], vv,
                            preferred_element_type=F32)
                o = t if o is None else o + t
            o = o / l
            o_pair = o if o_pair is None else jnp.where(sel, o, o_pair)
        o_ref[:, cols] = o_pair.astype(o_ref.dtype)


def neighbourhood_attention(za, rpb, batch, n):
    t = za.shape[0]
    bt = NA_BLOCK_TOKENS
    nblk = n // bt
    assert n % bt == 0 and nblk >= 3
    bias = na_bias_table(rpb)

    def qmap(b, i):
        return (b * nblk + i, 0)

    def kmap(d, col):
        return lambda b, i: (b * nblk + jnp.clip(i + d, 0, nblk - 1), col)

    def bmap(b, i):
        return (jnp.where(i == 0, 0, jnp.where(i == nblk - 1, 2, 1)), 0, 0, 0)

    def blk(m):
        return pl.BlockSpec((bt, NA_WIDTH), m)

    return pl.pallas_call(
        _na_body, grid=(batch, nblk),
        in_specs=[blk(qmap), blk(kmap(-1, 1)), blk(kmap(0, 1)), blk(kmap(1, 1)),
                  blk(kmap(-1, 2)), blk(kmap(0, 2)), blk(kmap(1, 2)),
                  pl.BlockSpec((1, NA_HEADS, bt, 3 * bt), bmap)],
        out_specs=blk(qmap),
        out_shape=jax.ShapeDtypeStruct((t, NA_WIDTH), BF16),
        compiler_params=_params("parallel", "parallel"), name="neighbourhood_attention",
    )(za, za, za, za, za, za, za, bias)


def _split_dot(a_bf, x, dims):
    hi = x.astype(BF16)
    lo = (x - hi.astype(F32)).astype(BF16)
    return (lax.dot_general(a_bf, hi, dims, preferred_element_type=F32)
            + lax.dot_general(a_bf, lo, dims, preferred_element_type=F32))


def _gla_body(qf_ref, kf_ref, vf_ref, gf_ref, qb_ref, kb_ref, vb_ref, gb_ref,
              of_ref, ob_ref, sf_ref, sb_ref):
    c = GLA_CHUNK
    blk = GLA_BLOCK
    grp = GLA_GROUP
    nchunk = blk // c

    @pl.when(pl.program_id(1) == 0)
    def _():
        sf_ref[...] = jnp.zeros_like(sf_ref)
        sb_ref[...] = jnp.zeros_like(sb_ref)

    row = lax.broadcasted_iota(jnp.int32, (blk, blk), 0)
    col = lax.broadcasted_iota(jnp.int32, (blk, blk), 1)
    same_chunk = (row // c) == (col // c)
    grow = lax.broadcasted_iota(jnp.int32, (grp, grp), 0)
    gcol = lax.broadcasted_iota(jnp.int32, (grp, grp), 1)
    same_chunk_g = (grow // c) == (gcol // c)
    mm = (((1,), (0,)), ((), ()))
    nt = (((1,), (1,)), ((), ()))
    tn = (((0,), (0,)), ((), ()))

    def prepare(q_ref, k_ref, v_ref, g_ref, reverse):
        keep_blk = same_chunk & ((row <= col) if reverse else (row >= col))
        keep_grp = same_chunk_g & ((grow <= gcol) if reverse else (grow >= gcol))
        g = g_ref[...]
        cum = _split_dot(keep_blk.astype(BF16), g, mm)
        edge = 0 if reverse else c - 1
        tot = cum.reshape(nchunk, c, GLA_KW)[:, edge:edge + 1, :]
        tot_rows = jnp.broadcast_to(tot, (nchunk, c, GLA_KW)).reshape(blk, GLA_KW)
        q = q_ref[...] * (GLA_DK ** -0.5)
        k = k_ref[...]
        q_t = (q * jnp.exp(cum)).astype(BF16)
        k_t = (k * jnp.exp(-cum)).astype(BF16)
        k_d = (k * jnp.exp(tot_rows - cum)).astype(BF16)
        v = v_ref[...].astype(BF16)
        intra = []
        for gi in range(blk // grp):
            rows = slice(gi * grp, (gi + 1) * grp)
            per_head = []
            for h in range(GLA_HEADS):
                kc = slice(h * GLA_DK, (h + 1) * GLA_DK)
                vc = slice(h * GLA_DV, (h + 1) * GLA_DV)
                a = lax.dot_general(q_t[rows, kc], k_t[rows, kc], nt, preferred_element_type=F32)
                a = jnp.where(keep_grp, a, 0.0).astype(BF16)
                per_head.append(jnp.dot(a, v[rows, vc], preferred_element_type=F32))
            intra.append(jnp.concatenate(per_head, axis=1))
        return jnp.concatenate(intra, axis=0), q_t, k_d, v, jnp.exp(tot)

    dirs = [(prepare(qf_ref, kf_ref, vf_ref, gf_ref, False), of_ref, sf_ref, False),
            (prepare(qb_ref, kb_ref, vb_ref, gb_ref, True), ob_ref, sb_ref, True)]
    states = [[s_ref[h] for h in range(GLA_HEADS)] for _, _, s_ref, _ in dirs]
    for step in range(nchunk):
        for d, ((intra, q_t, k_d, v, decay), o_ref, _, reverse) in enumerate(dirs):
            j = nchunk - 1 - step if reverse else step
            rows = slice(j * c, (j + 1) * c)
            outs = []
            for h in range(GLA_HEADS):
                kc = slice(h * GLA_DK, (h + 1) * GLA_DK)
                vc = slice(h * GLA_DV, (h + 1) * GLA_DV)
                st = states[d][h]
                outs.append(lax.dot_general(q_t[rows, kc], st.astype(BF16), nt, preferred_element_type=F32))
                ds = lax.dot_general(v[rows, vc], k_d[rows, kc], tn, preferred_element_type=F32)
                states[d][h] = st * decay[j, :, kc] + ds
            o_ref[rows, :] = intra[rows, :] + jnp.concatenate(outs, axis=1)
    for d, (_, _, s_ref, _) in enumerate(dirs):
        for h in range(GLA_HEADS):
            s_ref[h] = states[d][h]


def gla_bidirectional(zb, batch, n):
    t = zb.shape[0]
    nb = n // GLA_BLOCK
    assert n % GLA_BLOCK == 0

    def fwd(col):
        return lambda b, i: (b * nb + i, col)

    def bwd(col):
        return lambda b, i: (b * nb + nb - 1 - i, col)

    def kw(m):
        return pl.BlockSpec((GLA_BLOCK, GLA_KW), m)

    def vw(m):
        return pl.BlockSpec((GLA_BLOCK, GLA_VW), m)

    return pl.pallas_call(
        _gla_body, grid=(batch, nb),
        in_specs=[kw(fwd(0)), kw(fwd(1)), vw(fwd(1)), kw(fwd(6)),
                  kw(bwd(0)), kw(bwd(1)), vw(bwd(1)), kw(bwd(7))],
        out_specs=[vw(fwd(0)), vw(bwd(0))],
        out_shape=[jax.ShapeDtypeStruct((t, GLA_VW), F32)] * 2,
        scratch_shapes=[pltpu.VMEM((GLA_HEADS, GLA_DV, GLA_DK), F32)] * 2,
        compiler_params=_params("parallel", "arbitrary"), name="gla_bidirectional",
    )(zb, zb, zb, zb, zb, zb, zb, zb)


def _silu(x):
    return x * jax.nn.sigmoid(x)


def _out_ab_body(oa_ref, of_ref, ob_ref, r_ref, gn_ref, wa_ref, wb_ref, x_ref, o_ref):
    o = of_ref[...] + ob_ref[...]
    r = r_ref[...]
    parts = []
    for h in range(GLA_HEADS):
        vc = slice(h * GLA_DV, (h + 1) * GLA_DV)
        parts.append((_rms(o[:, vc], gn_ref[...]) * _silu(r[:, vc])).astype(BF16))
    o_b = jnp.concatenate(parts, axis=1)
    o_ref[...] = (jnp.dot(oa_ref[...], wa_ref[...], preferred_element_type=F32)
                  + jnp.dot(o_b, wb_ref[...], preferred_element_type=F32) + x_ref[...])


def out_proj_ab(o_a, o_fw, o_bw, zb, gla_norm, w_out, x):
    t = x.shape[0]
    tm = ROW_TILE
    w = w_out.astype(BF16)
    return pl.pallas_call(
        _out_ab_body, grid=(t // tm,),
        in_specs=[_row_spec(tm, NA_WIDTH), _row_spec(tm, GLA_VW), _row_spec(tm, GLA_VW),
                  _row_spec(tm, GLA_VW, col=2), _full_spec((1, GLA_DV)),
                  _full_spec((NA_WIDTH, D_MODEL)), _full_spec((GLA_VW, D_MODEL)),
                  _row_spec(tm, D_MODEL)],
        out_specs=_row_spec(tm, D_MODEL),
        out_shape=jax.ShapeDtypeStruct((t, D_MODEL), F32),
        compiler_params=_params("parallel"), name="out_proj_ab",
    )(o_a, o_fw, o_bw, zb, gla_norm.reshape(1, -1), w[:NA_WIDTH], w[NA_WIDTH:], x)


def _in_c_body(x_ref, g_ref, wbg_ref, wcg_ref, wxt_ref, bg_ref, u_ref):
    hn = _rms(x_ref[...], g_ref[...]).astype(BF16)
    bg_ref[...] = jnp.dot(hn, wbg_ref[...], preferred_element_type=F32)
    u_ref[...] = (jnp.dot(hn, wcg_ref[...], preferred_element_type=F32)
                  * jnp.dot(hn, wxt_ref[...], preferred_element_type=F32))


def in_proj_c(x, gain, w_in):
    t = x.shape[0]
    tm = ROW_TILE
    w = w_in.astype(BF16)
    d = D_MODEL
    return pl.pallas_call(
        _in_c_body, grid=(t // tm,),
        in_specs=[_row_spec(tm, d), _full_spec((1, d))] + [_full_spec((d, d))] * 3,
        out_specs=[_row_spec(tm, d)] * 2,
        out_shape=[jax.ShapeDtypeStruct((t, d), F32)] * 2,
        compiler_params=_params("parallel"), name="in_proj_c",
    )(x, gain.reshape(1, -1), w[:, :d], w[:, d:2 * d], w[:, 2 * d:])


def _conv_out_body(u_ref, up_ref, un_ref, bg_ref, cw_ref, w_ref, x_ref, o_ref, *, tiles_per_seq):
    i = pl.program_id(0)
    tm = u_ref.shape[0]
    u = u_ref[...]
    row = lax.broadcasted_iota(jnp.int32, (tm, 1), 0)
    first = (i % tiles_per_seq) == 0
    last = (i % tiles_per_seq) == tiles_per_seq - 1
    prev_row = jnp.where(first, 0.0, up_ref[SUBLANES - 1:SUBLANES, :])
    next_row = jnp.where(last, 0.0, un_ref[0:1, :])
    u_prev = jnp.where(row == 0, prev_row, pltpu.roll(u, 1, axis=0))
    u_next = jnp.where(row == tm - 1, next_row, pltpu.roll(u, tm - 1, axis=0))
    conv = cw_ref[0:1, :] * u_prev + cw_ref[1:2, :] * u + cw_ref[2:3, :] * u_next
    o_ref[...] = (jnp.dot((bg_ref[...] * conv).astype(BF16), w_ref[...], preferred_element_type=F32)
                  + x_ref[...])


def conv_out_proj(u, bg, conv_w, w_out, x, n):
    t = x.shape[0]
    tm = ROW_TILE
    d = D_MODEL
    per = tm // SUBLANES
    nhalo = t // SUBLANES
    assert n % tm == 0
    return pl.pallas_call(
        functools.partial(_conv_out_body, tiles_per_seq=n // tm), grid=(t // tm,),
        in_specs=[_row_spec(tm, d),
                  pl.BlockSpec((SUBLANES, d), lambda i: (jnp.maximum(i * per - 1, 0), 0)),
                  pl.BlockSpec((SUBLANES, d), lambda i: (jnp.minimum((i + 1) * per, nhalo - 1), 0)),
                  _row_spec(tm, d), _full_spec((CONV_W, d)), _full_spec((d, d)), _row_spec(tm, d)],
        out_specs=_row_spec(tm, d),
        out_shape=jax.ShapeDtypeStruct((t, d), F32),
        compiler_params=_params("parallel"), name="conv_out_proj",
    )(u, u, u, bg, conv_w, w_out.astype(BF16), x)


def _mem_body(x_ref, g_ref, wq_ref, k_ref, v_ref, wo_ref, o_ref):
    x = x_ref[...]
    hn = _rms(x, g_ref[...]).astype(BF16)
    q = (jnp.dot(hn, wq_ref[...], preferred_element_type=F32) * MEM_HEAD_DIM ** -0.5).astype(BF16)
    nt = (((1,), (1,)), ((), ()))
    parts = []
    for h in range(MEM_HEADS):
        hc = slice(h * MEM_HEAD_DIM, (h + 1) * MEM_HEAD_DIM)
        s = lax.dot_general(q[:, hc], k_ref[0, :, hc], nt, preferred_element_type=F32)
        p = jnp.exp(s - jnp.max(s, axis=-1, keepdims=True))
        l = jnp.sum(p, axis=-1, keepdims=True)
        o = jnp.dot(p.astype(BF16), v_ref[0, :, hc], preferred_element_type=F32) / l
        parts.append(o.astype(BF16))
    o_ref[...] = jnp.dot(jnp.concatenate(parts, axis=1), wo_ref[...], preferred_element_type=F32) + x


def memory_attention(x, mem, gain, wq, wk, wv, wo, n):
    t, d = x.shape
    b, nm, _ = mem.shape
    tm = ROW_TILE
    per = n // tm
    memf = mem.reshape(b * nm, d)
    k = matmul(memf, wk.astype(BF16), BF16).reshape(b, nm, d)
    v = matmul(memf, wv.astype(BF16), BF16).reshape(b, nm, d)
    kv_spec = pl.BlockSpec((1, nm, d), lambda i: (i // per, 0, 0))
    return pl.pallas_call(
        _mem_body, grid=(t // tm,),
        in_specs=[_row_spec(tm, d), _full_spec((1, d)), _full_spec((d, d)), kv_spec, kv_spec,
                  _full_spec((d, d))],
        out_specs=_row_spec(tm, d),
        out_shape=jax.ShapeDtypeStruct((t, d), F32),
        compiler_params=_params("parallel"), name="memory_attention",
    )(x, gain.reshape(1, -1), wq.astype(BF16), k, v, wo.astype(BF16))


def _router_body(x_ref, g_ref, r_ref, aff_ref):
    hn = _rms(x_ref[...], g_ref[...])
    r = r_ref[...]
    hn_hi, r_hi = hn.astype(BF16), r.astype(BF16)
    hn_lo, r_lo = (hn - hn_hi.astype(F32)).astype(BF16), (r - r_hi.astype(F32)).astype(BF16)
    nt = (((1,), (1,)), ((), ()))
    logits = (lax.dot_general(r_hi, hn_hi, nt, preferred_element_type=F32)
              + lax.dot_general(r_hi, hn_lo, nt, preferred_element_type=F32)
              + lax.dot_general(r_lo, hn_hi, nt, preferred_element_type=F32))
    p = jnp.exp(logits - jnp.max(logits, axis=0, keepdims=True))
    aff_ref[...] = p / jnp.sum(p, axis=0, keepdims=True)


def router_affinity(x, gain, router):
    t, d = x.shape
    tm = ROW_TILE
    return pl.pallas_call(
        _router_body, grid=(t // tm,),
        in_specs=[_row_spec(tm, d), _full_spec((1, d)), _full_spec((N_EXPERTS, d))],
        out_specs=pl.BlockSpec((N_EXPERTS, tm), lambda i: (0, i)),
        out_shape=jax.ShapeDtypeStruct((N_EXPERTS, t), F32),
        compiler_params=_params("parallel"), name="router_affinity",
    )(x, gain.reshape(1, -1), router.T)


def _select_body(aff_ref, slot_ref, base_ref, cnt_ref, *, cap):
    e, nc, _ = aff_ref.shape
    per = MOE_TILE // LANES
    bits = pltpu.bitcast(aff_ref[...], jnp.int32)

    def count(mask):
        s = jnp.sum(mask.astype(jnp.int32), axis=1, keepdims=True)
        return jnp.sum(s, axis=2, keepdims=True)

    def bisect(b, thr):
        cand = thr | (jnp.int32(1) << (30 - b))
        return jnp.where(count(bits >= cand) >= cap, cand, thr)

    thr = lax.fori_loop(0, 31, bisect, jnp.zeros((e, 1, 1), jnp.int32))
    gt = bits > thr
    eq = bits == thr
    need_eq = cap - count(gt)

    li = lax.broadcasted_iota(jnp.int32, (LANES, LANES), 0)
    lj = lax.broadcasted_iota(jnp.int32, (LANES, LANES), 1)
    incl = (li <= lj).astype(BF16)
    ci = lax.broadcasted_iota(jnp.int32, (nc, nc), 0)
    cj = lax.broadcasted_iota(jnp.int32, (nc, nc), 1)
    strict = (cj < ci).astype(BF16)
    same_tile = (cj // per == ci // per).astype(BF16)
    same_tile_before = ((cj // per == ci // per) & (cj < ci)).astype(BF16)
    tiles_before = ((cj // per < ci // per) & (cj % per == 0)).astype(BF16)

    def chunk_prefix(m):
        within = jnp.dot(m, incl, preferred_element_type=F32)
        tot = jnp.broadcast_to(within[:, LANES - 1:LANES], (nc, LANES)).astype(BF16)
        return within, tot

    for x in range(e):
        eq_x = eq[x].astype(BF16)
        within, tot = chunk_prefix(eq_x)
        rank_eq = within - eq_x.astype(F32) + jnp.dot(strict, tot, preferred_element_type=F32)
        sel = gt[x] | (eq[x] & (rank_eq < need_eq[x].astype(F32)))
        sel_b = sel.astype(BF16)
        within, tot = chunk_prefix(sel_b)
        tile_cnt = jnp.dot(same_tile, tot, preferred_element_type=F32).astype(jnp.int32)
        base = jnp.dot(tiles_before, tile_cnt.astype(BF16), preferred_element_type=F32)
        in_tile = jnp.dot(same_tile_before, tot, preferred_element_type=F32)
        pos = (base + in_tile + within - sel_b.astype(F32)).astype(jnp.int32)
        slot_ref[x] = jnp.where(sel, pos, NOT_SELECTED)
        base_ref[x] = base.astype(jnp.int32)
        cnt_ref[x] = tile_cnt


def select_slots(aff_t, cap):
    e, t = aff_t.shape
    nc = t // LANES
    per = MOE_TILE // LANES
    slot, base, cnt = pl.pallas_call(
        functools.partial(_select_body, cap=cap),
        out_shape=[jax.ShapeDtypeStruct((e, nc, LANES), jnp.int32)] * 3,
        compiler_params=pltpu.CompilerParams(vmem_limit_bytes=VMEM_LIMIT_BYTES), name="select_slots",
    )(aff_t.reshape(e, nc, LANES))
    return slot.reshape(e, t), base[:, ::per, 0].reshape(-1), cnt[:, ::per, 0].reshape(-1)


def _one_hot_t(slot_ref, start, width, lo=None, hi=None):
    tm = slot_ref.shape[1]
    w = lax.broadcasted_iota(jnp.int32, (width, tm), 0)
    rows = []
    for e in range(N_EXPERTS):
        s = slot_ref[e:e + 1, :]
        hit = (s - start[e]) == w
        if lo is not None:
            hit = hit & (s >= lo[e])
        if hi is not None:
            hit = hit & (s < hi[e])
        rows.append(hit)
    return jnp.concatenate(rows, axis=0)


def _tile_scalars(base_ref, cnt_ref, i, nt):
    lo = [base_ref[e * nt + i] for e in range(N_EXPERTS)]
    hi = [lo[e] + cnt_ref[e * nt + i] for e in range(N_EXPERTS)]
    start = [(l // SLAB_ALIGN) * SLAB_ALIGN for l in lo]
    return start, hi


def _rounds(start, hi):
    m = hi[0] - start[0]
    for e in range(1, N_EXPERTS):
        m = jnp.maximum(m, hi[e] - start[e])
    return (m + MOE_SLAB - 1) // MOE_SLAB


def _dispatch_body(base_ref, cnt_ref, x_ref, g_ref, slot_ref, aff_ref, xe_ref, ge_ref,
                   xbuf, gbuf, xover, gover, xcarry, gcarry, sem, osem, *, nt, cap):
    i = pl.program_id(0)
    cur = i % 2
    w_ = MOE_SLAB
    al = SLAB_ALIGN
    hn = _rms(x_ref[...], g_ref[...]).astype(BF16)
    start, hi = _tile_scalars(base_ref, cnt_ref, i, nt)
    nxt = [(h // al) * al for h in hi]

    @pl.when(i == 0)
    def _():
        xcarry[...] = jnp.zeros_like(xcarry)
        gcarry[...] = jnp.zeros_like(gcarry)

    def gather(hot):
        rows = jnp.dot(hot.astype(BF16), hn, preferred_element_type=F32)
        hot_f = hot.astype(F32)
        n = hot.shape[0] // N_EXPERTS
        gates = [jnp.sum(hot_f[e * n:(e + 1) * n] * aff_ref[e:e + 1, :], axis=1, keepdims=True)
                 for e in range(N_EXPERTS)]
        return rows, gates

    def copies(xsrc, gsrc, st, e, xs, gs):
        dst = pl.ds(pl.multiple_of(st[e], al), w_)
        return (pltpu.make_async_copy(xsrc.at[e], xe_ref.at[e, dst], xs),
                pltpu.make_async_copy(gsrc.at[e], ge_ref.at[e, dst], gs))

    rows, gates = gather(_one_hot_t(slot_ref, start, w_))
    keep = []
    for e in range(N_EXPERTS):
        xc = xcarry[e].astype(F32)
        gc = gcarry[e]
        win = rows[e * w_:(e + 1) * w_]
        xbuf[cur, e] = jnp.concatenate([win[:al] + xc, win[al:]], axis=0).astype(BF16)
        gwin = jnp.broadcast_to(gates[e], (w_, LANES))
        gbuf[cur, e] = jnp.concatenate([gwin[:al] + gc, gwin[al:]], axis=0)
        keep.append((xc, gc))
    rows, gates = gather(_one_hot_t(slot_ref, nxt, al))
    for e in range(N_EXPERTS):
        same = (nxt[e] == start[e]).astype(F32)
        xcarry[e] = (rows[e * al:(e + 1) * al] + same * keep[e][0]).astype(BF16)
        gcarry[e] = jnp.broadcast_to(gates[e], (al, LANES)) + same * keep[e][1]

    @pl.when(i > 0)
    def _():
        st_p, _ = _tile_scalars(base_ref, cnt_ref, i - 1, nt)
        for e in range(N_EXPERTS):
            for cp in copies(xbuf.at[1 - cur], gbuf.at[1 - cur], st_p, e, sem.at[1 - cur, 0], sem.at[1 - cur, 1]):
                cp.wait()

    for e in range(N_EXPERTS):
        for cp in copies(xbuf.at[cur], gbuf.at[cur], start, e, sem.at[cur, 0], sem.at[cur, 1]):
            cp.start()

    def extra_round(k, carry):
        st_k = [s + k * w_ for s in start]
        rows, gates = gather(_one_hot_t(slot_ref, st_k, w_))
        for e in range(N_EXPERTS):
            xover[e] = rows[e * w_:(e + 1) * w_].astype(BF16)
            gover[e] = jnp.broadcast_to(gates[e], (w_, LANES))
        for e in range(N_EXPERTS):
            @pl.when(hi[e] - start[e] > k * w_)
            def _():
                for cp in copies(xover, gover, st_k, e, osem.at[0], osem.at[1]):
                    cp.start()
        for e in range(N_EXPERTS):
            @pl.when(hi[e] - start[e] > k * w_)
            def _():
                for cp in copies(xover, gover, st_k, e, osem.at[0], osem.at[1]):
                    cp.wait()
        return carry

    lax.fori_loop(1, _rounds(start, hi), extra_round, 0)

    @pl.when(i == nt - 1)
    def _():
        for e in range(N_EXPERTS):
            for cp in copies(xbuf.at[cur], gbuf.at[cur], start, e, sem.at[cur, 0], sem.at[cur, 1]):
                cp.wait()
        for e in range(N_EXPERTS):
            xover[e] = jnp.zeros((w_, xover.shape[-1]), BF16)
            gover[e] = jnp.zeros((w_, LANES), F32)
        tail = [cap] * N_EXPERTS
        for e in range(N_EXPERTS):
            for cp in copies(xover, gover, tail, e, osem.at[0], osem.at[1]):
                cp.start()
        for e in range(N_EXPERTS):
            for cp in copies(xover, gover, tail, e, osem.at[0], osem.at[1]):
                cp.wait()


def dispatch(x, gain, slot, aff_t, base, cnt, cap):
    t, d = x.shape
    tm = MOE_TILE
    nt = t // tm
    rows = cap + MOE_SLAB
    any_spec = pl.BlockSpec(memory_space=pl.ANY)
    grid_spec = pltpu.PrefetchScalarGridSpec(
        num_scalar_prefetch=2, grid=(nt,),
        in_specs=[pl.BlockSpec((tm, d), lambda i, *_: (i, 0)), pl.BlockSpec((1, d), lambda i, *_: (0, 0)),
                  pl.BlockSpec((N_EXPERTS, tm), lambda i, *_: (0, i)),
                  pl.BlockSpec((N_EXPERTS, tm), lambda i, *_: (0, i))],
        out_specs=[any_spec, any_spec],
        scratch_shapes=[pltpu.VMEM((2, N_EXPERTS, MOE_SLAB, d), BF16),
                        pltpu.VMEM((2, N_EXPERTS, MOE_SLAB, LANES), F32),
                        pltpu.VMEM((N_EXPERTS, MOE_SLAB, d), BF16),
                        pltpu.VMEM((N_EXPERTS, MOE_SLAB, LANES), F32),
                        pltpu.VMEM((N_EXPERTS, SLAB_ALIGN, d), BF16),
                        pltpu.VMEM((N_EXPERTS, SLAB_ALIGN, LANES), F32),
                        pltpu.SemaphoreType.DMA((2, 2)), pltpu.SemaphoreType.DMA((2,))])
    return pl.pallas_call(
        functools.partial(_dispatch_body, nt=nt, cap=cap), grid_spec=grid_spec,
        out_shape=[jax.ShapeDtypeStruct((N_EXPERTS, rows, d), BF16),
                   jax.ShapeDtypeStruct((N_EXPERTS, rows, LANES), F32)],
        compiler_params=_params("arbitrary"), name="moe_dispatch",
    )(base, cnt, x, gain.reshape(1, -1), slot, aff_t)


def _ffn_body(xe_ref, gate_ref, wg_ref, wu_ref, wd_ref, o_ref):
    xe = xe_ref[0]
    a = jnp.dot(xe, wg_ref[0, 0], preferred_element_type=F32)
    b = jnp.dot(xe, wu_ref[0, 0], preferred_element_type=F32)
    hid = (_silu(a) * b).astype(BF16)
    o_ref[0] = (jnp.dot(hid, wd_ref[0, 0], preferred_element_type=F32) * gate_ref[0][:, :1]).astype(o_ref.dtype)


def expert_ffn(xe, gate, wg, wu, wd, layer, cap):
    e, _, d = xe.shape
    f = wg.shape[-1]
    tm = FFN_TILE
    return pl.pallas_call(
        _ffn_body, grid=(e, cap // tm),
        in_specs=[
            pl.BlockSpec((1, tm, d), lambda i, j: (i, j, 0)),
            pl.BlockSpec((1, tm, LANES), lambda i, j: (i, j, 0)),
            pl.BlockSpec((1, 1, d, f), lambda i, j: (layer, i, 0, 0)),
            pl.BlockSpec((1, 1, d, f), lambda i, j: (layer, i, 0, 0)),
            pl.BlockSpec((1, 1, f, d), lambda i, j: (layer, i, 0, 0)),
        ],
        out_specs=pl.BlockSpec((1, tm, d), lambda i, j: (i, j, 0)),
        out_shape=jax.ShapeDtypeStruct((e, cap, d), BF16),
        compiler_params=_params("parallel", "parallel"), name="expert_ffn",
    )(xe, gate, wg, wu, wd)


def _combine_body(base_ref, cnt_ref, x_ref, slot_ref, y_ref, g_ref, o_ref, ybuf, yover, sem, osem,
                  *, nt, cap, out_norm):
    i = pl.program_id(0)
    cur = i % 2
    w_ = MOE_SLAB
    tn = (((0,), (0,)), ((), ()))

    def window(st):
        return [pl.multiple_of(jnp.minimum(s, cap - w_), SLAB_ALIGN) for s in st]

    def fetch(step, buf, xs):
        st, _ = _tile_scalars(base_ref, cnt_ref, step, nt)
        ws = window(st)
        return [pltpu.make_async_copy(y_ref.at[e, pl.ds(ws[e], w_)], buf.at[e], xs)
                for e in range(N_EXPERTS)]

    @pl.when(i == 0)
    def _():
        yover[...] = jnp.zeros_like(yover)
        for cp in fetch(0, ybuf.at[0], sem.at[0]):
            cp.start()

    @pl.when(i + 1 < nt)
    def _():
        for cp in fetch(i + 1, ybuf.at[1 - cur], sem.at[1 - cur]):
            cp.start()

    for cp in fetch(i, ybuf.at[cur], sem.at[cur]):
        cp.wait()

    start, hi = _tile_scalars(base_ref, cnt_ref, i, nt)

    def scatter(buf, st_k, first):
        hot = _one_hot_t(slot_ref, window(st_k), w_, lo=None if first else st_k, hi=[s + w_ for s in st_k])
        ys = buf[...].reshape(N_EXPERTS * w_, buf.shape[-1])
        return lax.dot_general(hot.astype(BF16), ys, tn, preferred_element_type=F32)

    o_ref[...] = x_ref[...] + scatter(ybuf.at[cur], start, True)

    def extra_round(k, carry):
        st_k = [s + k * w_ for s in start]
        ws = window(st_k)
        for e in range(N_EXPERTS):
            @pl.when(hi[e] - start[e] > k * w_)
            def _():
                pltpu.make_async_copy(y_ref.at[e, pl.ds(ws[e], w_)], yover.at[e], osem).start()
        for e in range(N_EXPERTS):
            @pl.when(hi[e] - start[e] > k * w_)
            def _():
                pltpu.make_async_copy(y_ref.at[e, pl.ds(ws[e], w_)], yover.at[e], osem).wait()
        o_ref[...] += scatter(yover, st_k, False)
        return carry

    lax.fori_loop(1, _rounds(start, hi), extra_round, 0)
    if out_norm:
        o_ref[...] = _rms(o_ref[...], g_ref[...])


def combine(x, slot, y, base, cnt, cap, out_gain, out_norm):
    t, d = x.shape
    tm = MOE_TILE
    nt = t // tm
    grid_spec = pltpu.PrefetchScalarGridSpec(
        num_scalar_prefetch=2, grid=(nt,),
        in_specs=[pl.BlockSpec((tm, d), lambda i, *_: (i, 0)),
                  pl.BlockSpec((N_EXPERTS, tm), lambda i, *_: (0, i)),
                  pl.BlockSpec(memory_space=pl.ANY),
                  pl.BlockSpec((1, d), lambda i, *_: (0, 0))],
        out_specs=pl.BlockSpec((tm, d), lambda i, *_: (i, 0)),
        scratch_shapes=[pltpu.VMEM((2, N_EXPERTS, MOE_SLAB, d), BF16),
                        pltpu.VMEM((N_EXPERTS, MOE_SLAB, d), BF16),
                        pltpu.SemaphoreType.DMA((2,)), pltpu.SemaphoreType.DMA(())])
    return pl.pallas_call(
        functools.partial(_combine_body, nt=nt, cap=cap, out_norm=out_norm), grid_spec=grid_spec,
        out_shape=jax.ShapeDtypeStruct((t, d), F32),
        compiler_params=_params("arbitrary"), name="moe_combine",
    )(base, cnt, x, slot, y, out_gain.reshape(1, -1))


def expert_choice_ffn(x, layer, p):
    t, _ = x.shape
    gain = p['norm_ffn'][layer]
    cap = EC_CAPACITY * t // N_EXPERTS
    aff_t = router_affinity(x, gain, p['router'][layer])
    slot, base, cnt = select_slots(aff_t, cap)
    xe, ge = dispatch(x, gain, slot, aff_t, base, cnt, cap)
    y = expert_ffn(xe, ge, p['w_gate'], p['w_up'], p['w_down'], layer, cap)
    return combine(x, slot, y, base, cnt, cap, p['final_norm'], layer == DEPTH - 1)


def trunk(x, mem, p):
    b, n, d = x.shape
    x = x.reshape(b * n, d)
    for layer in range(DEPTH):
        i = layer // 2
        if layer % 2 == 0:
            za, zb = in_proj_ab(x, p['norm_mix'][layer], p['w_in_ab'][i], p['gla_gw_f'][i],
                                p['gla_gb_f'][i], p['gla_gw_b'][i], p['gla_gb_b'][i])
            o_a = neighbourhood_attention(za, p['na_rpb'][i], b, n)
            o_fw, o_bw = gla_bidirectional(zb, b, n)
            x = out_proj_ab(o_a, o_fw, o_bw, zb, p['gla_norm'][i], p['w_out_ab'][i], x)
        else:
            bg, u = in_proj_c(x, p['norm_mix'][layer], p['w_in_c'][i])
            x = conv_out_proj(u, bg, p['conv_w'][i], p['w_out_c'][i], x, n)
        x = memory_attention(x, mem, p['norm_mem'][layer], p['w_mq'][layer], p['w_mk'][layer],
                             p['w_mv'][layer], p['w_mo'][layer], n)
        x = expert_choice_ffn(x, layer, p)
    return x.reshape(b, n, d)


def kernel(x_prompt, x_sample, mem_prompt, mem_sample, w_in_ab, na_rpb, gla_gw_f, gla_gb_f, gla_gw_b,
           gla_gb_b, gla_norm, w_out_ab, w_in_c, conv_w, w_out_c, norm_mix, norm_mem, norm_ffn,
           w_mq, w_mk, w_mv, w_mo, router, w_gate, w_up, w_down, final_norm):
    p = dict(w_in_ab=w_in_ab, na_rpb=na_rpb, gla_gw_f=gla_gw_f, gla_gb_f=gla_gb_f, gla_gw_b=gla_gw_b,
             gla_gb_b=gla_gb_b, gla_norm=gla_norm, w_out_ab=w_out_ab, w_in_c=w_in_c, conv_w=conv_w,
             w_out_c=w_out_c, norm_mix=norm_mix, norm_mem=norm_mem, norm_ffn=norm_ffn, w_mq=w_mq,
             w_mk=w_mk, w_mv=w_mv, w_mo=w_mo, router=router, w_gate=w_gate.astype(BF16),
             w_up=w_up.astype(BF16), w_down=w_down.astype(BF16), final_norm=final_norm)
    y_prompt = trunk(x_prompt, mem_prompt, p)
    y_sample = trunk(x_sample, mem_sample, p)
    return (y_prompt, y_sample)
```

```python
import functools

import jax
import jax.numpy as jnp
import numpy as np
from jax import lax
from jax.experimental import pallas as pl
from jax.experimental.pallas import tpu as pltpu

D_MODEL = 1024
DEPTH = 4
GRID_W = 64
EPS = 1e-6
NEG_INF = -1e30
NA_HEADS = 8
NA_HEAD_DIM = 64
NA_WIN_R = 8
NA_WIN_C = 16
NA_WIDTH = NA_HEADS * NA_HEAD_DIM
GLA_HEADS = 4
GLA_DK = 64
GLA_DV = 128
GLA_RANK = 16
GLA_TAU = 16.0
GLA_CHUNK = 64
GLA_KW = GLA_HEADS * GLA_DK
GLA_VW = GLA_HEADS * GLA_DV
CONV_W = 3
MEM_HEADS = 4
MEM_HEAD_DIM = D_MODEL // MEM_HEADS
N_EXPERTS = 16
D_EXPERT = 2 * D_MODEL
EC_CAPACITY = 2

VMEM_LIMIT_BYTES = 48 * 1024 * 1024
ROW_TILE = 512
SUBLANES = 8
NA_ROWS_PER_BLOCK = 4
NA_BLOCK_TOKENS = NA_ROWS_PER_BLOCK * GRID_W
GLA_BLOCK = 512
GLA_GROUP = 256
LANES = 128
NOT_SELECTED = -(1 << 24)
MOE_TILE = 256
MOE_SLAB = 64
SLAB_ALIGN = 16
FFN_TILE = 512
BF16 = jnp.bfloat16
F32 = jnp.float32


def _params(*sem):
    return pltpu.CompilerParams(dimension_semantics=sem, vmem_limit_bytes=VMEM_LIMIT_BYTES)


def _rms(x, g):
    return x * lax.rsqrt(jnp.mean(x * x, axis=-1, keepdims=True) + EPS) * g


def _row_spec(tm, n, col=0):
    return pl.BlockSpec((tm, n), lambda i: (i, col))


def _full_spec(shape):
    return pl.BlockSpec(shape, lambda *_: (0,) * len(shape))


def _mm_body(x_ref, w_ref, o_ref):
    o_ref[...] = jnp.dot(x_ref[...].astype(BF16), w_ref[...],
                         preferred_element_type=F32).astype(o_ref.dtype)


def matmul(x, w, out_dtype):
    m, k = x.shape
    n = w.shape[1]
    tm = min(ROW_TILE, m)
    return pl.pallas_call(
        _mm_body, grid=(m // tm,),
        in_specs=[_row_spec(tm, k), _full_spec((k, n))],
        out_specs=_row_spec(tm, n),
        out_shape=jax.ShapeDtypeStruct((m, n), out_dtype),
        compiler_params=_params("parallel"), name="row_matmul",
    )(x, w)


def _log_sigmoid(x):
    return jnp.minimum(x, 0.0) - jnp.log1p(jnp.exp(-jnp.abs(x)))


def _in_ab_body(x_ref, g_ref, wa_ref, wb_ref, wg_ref, gw_ref, gb_ref, za_ref, zb_ref):
    hn = _rms(x_ref[...], g_ref[...]).astype(BF16)
    za_ref[...] = jnp.dot(hn, wa_ref[...], preferred_element_type=F32).astype(za_ref.dtype)
    nb = wb_ref.shape[1]
    zb_ref[:, :nb] = jnp.dot(hn, wb_ref[...], preferred_element_type=F32)
    lowrank = jnp.dot(hn, wg_ref[...], preferred_element_type=F32)
    hi = lowrank.astype(BF16)
    lo = (lowrank - hi.astype(F32)).astype(BF16)
    pre = jnp.dot(jnp.concatenate([hi, hi, lo], axis=1), gw_ref[...], preferred_element_type=F32) + gb_ref[...]
    zb_ref[:, nb:] = _log_sigmoid(pre) / GLA_TAU


def in_proj_ab(x, gain, w_in, gwf, gbf, gwb, gbb):
    t = x.shape[0]
    tm = ROW_TILE
    na_w = 3 * NA_WIDTH
    gl_w = 2 * GLA_KW + 2 * GLA_VW
    o = np.cumsum([0, NA_WIDTH, NA_WIDTH, NA_WIDTH, GLA_KW, GLA_KW, GLA_VW, GLA_RANK, GLA_RANK, GLA_VW])
    wa = w_in[:, :o[3]].astype(BF16)
    wb = jnp.concatenate([w_in[:, o[3]:o[6]], w_in[:, o[8]:o[9]]], axis=1).astype(BF16)
    wg = w_in[:, o[6]:o[8]].astype(BF16)
    zero = jnp.zeros_like(gwf)
    gw = jnp.concatenate([jnp.concatenate([gwf, zero], axis=1), jnp.concatenate([zero, gwb], axis=1)], axis=0)
    gw_hi = gw.astype(BF16)
    gw_lo = (gw - gw_hi.astype(F32)).astype(BF16)
    gw3 = jnp.concatenate([gw_hi, gw_lo, gw_hi], axis=0)
    gbias = jnp.concatenate([gbf, gbb]).reshape(1, -1)
    return pl.pallas_call(
        _in_ab_body, grid=(t // tm,),
        in_specs=[_row_spec(tm, D_MODEL), _full_spec((1, D_MODEL)), _full_spec((D_MODEL, na_w)),
                  _full_spec((D_MODEL, gl_w)), _full_spec((D_MODEL, 2 * GLA_RANK)),
                  _full_spec((6 * GLA_RANK, 2 * GLA_KW)), _full_spec((1, 2 * GLA_KW))],
        out_specs=[_row_spec(tm, na_w), _row_spec(tm, gl_w + 2 * GLA_KW)],
        out_shape=[jax.ShapeDtypeStruct((t, na_w), BF16),
                   jax.ShapeDtypeStruct((t, gl_w + 2 * GLA_KW), F32)],
        compiler_params=_params("parallel"), name="in_proj_ab",
    )(x, gain.reshape(1, -1), wa, wb, wg, gw3, gbias)


def na_bias_table(rpb):
    rb = NA_ROWS_PER_BLOCK
    a = np.arange(rb)[:, None, None, None]
    c = np.arange(GRID_W)[None, :, None, None]
    u = np.arange(3 * rb)[None, None, :, None]
    kc = np.arange(GRID_W)[None, None, None, :]
    wstart = np.clip(c - NA_WIN_C // 2, 0, GRID_W - NA_WIN_C)
    col_ok = (kc >= wstart) & (kc < wstart + NA_WIN_C)
    dc = np.clip(kc - c + NA_WIN_C - 1, 0, 2 * NA_WIN_C - 2)
    dr = np.clip(u - a + NA_WIN_R // 2 - 1, 0, 2 * NA_WIN_R - 2)
    row_ok = [
        (u >= rb) & (u < rb + NA_WIN_R) & (a >= 0),
        (u - a >= 0) & (u - a < NA_WIN_R),
        (u >= 0) & (u < NA_WIN_R) & (a >= 0),
    ]
    shape = (rb, GRID_W, 3 * rb, GRID_W)
    flat = (NA_BLOCK_TOKENS, 3 * NA_BLOCK_TOKENS)
    pick_r = jnp.asarray(np.eye(2 * NA_WIN_R - 1, dtype=np.float32)[dr[:, 0, :, 0]])
    pick_c = jnp.asarray(np.eye(2 * NA_WIN_C - 1, dtype=np.float32)[dc[0, :, 0, :]])
    vals = jnp.einsum('aud,hdp,ckp->hacuk', pick_r, rpb.astype(F32), pick_c,
                      precision=lax.Precision.HIGHEST).reshape((NA_HEADS,) + flat)
    out = []
    for ok in row_ok:
        m = np.broadcast_to(ok & col_ok, shape).reshape(flat)
        out.append(jnp.where(jnp.asarray(m)[None], vals, NEG_INF))
    return jnp.stack(out)


def _na_body(q_ref, kp_ref, kc_ref, kn_ref, vp_ref, vc_ref, vn_ref, bias_ref, o_ref):
    lane = lax.broadcasted_iota(jnp.int32, (1, 2 * NA_HEAD_DIM), 1)
    scale = NA_HEAD_DIM ** -0.5
    nt = (((1,), (1,)), ((), ()))
    for hp in range(NA_HEADS // 2):
        cols = slice(hp * 2 * NA_HEAD_DIM, (hp + 1) * 2 * NA_HEAD_DIM)
        q = q_ref[:, cols] * scale
        ks = [r[:, cols] for r in (kp_ref, kc_ref, kn_ref)]
        vs = [r[:, cols] for r in (vp_ref, vc_ref, vn_ref)]
        o_pair = None
        for sub in range(2):
            sel = (lane < NA_HEAD_DIM) if sub == 0 else (lane >= NA_HEAD_DIM)
            qm = jnp.where(sel, q, jnp.zeros_like(q))
            s = jnp.concatenate(
                [lax.dot_general(qm, kk, nt, preferred_element_type=F32) for kk in ks], axis=1)
            s = s + bias_ref[0, 2 * hp + sub]
            p = jnp.exp(s - jnp.max(s, axis=-1, keepdims=True))
            l = jnp.sum(p, axis=-1, keepdims=True)
            pb = p.astype(BF16)
            o = None
            for j, vv in enumerate(vs):
                t = jnp.dot(pb[:, j * NA_BLOCK_TOKENS:(j + 1) * NA_BLOCK_TOKENS], vv,
                            preferred_element_type=F32)
                o = t if o is None else o + t
            o = o / l
            o_pair = o if o_pair is None else jnp.where(sel, o, o_pair)
        o_ref[:, cols] = o_pair.astype(o_ref.dtype)


def neighbourhood_attention(za, rpb, batch, n):
    t = za.shape[0]
    bt = NA_BLOCK_TOKENS
    nblk = n // bt
    assert n % bt == 0 and nblk >= 3
    bias = na_bias_table(rpb)

    def qmap(b, i):
        return (b * nblk + i, 0)

    def kmap(d, col):
        return lambda b, i: (b * nblk + jnp.clip(i + d, 0, nblk - 1), col)

    def bmap(b, i):
        return (jnp.where(i == 0, 0, jnp.where(i == nblk - 1, 2, 1)), 0, 0, 0)

    def blk(m):
        return pl.BlockSpec((bt, NA_WIDTH), m)

    return pl.pallas_call(
        _na_body, grid=(batch, nblk),
        in_specs=[blk(qmap), blk(kmap(-1, 1)), blk(kmap(0, 1)), blk(kmap(1, 1)),
                  blk(kmap(-1, 2)), blk(kmap(0, 2)), blk(kmap(1, 2)),
                  pl.BlockSpec((1, NA_HEADS, bt, 3 * bt), bmap)],
        out_specs=blk(qmap),
        out_shape=jax.ShapeDtypeStruct((t, NA_WIDTH), BF16),
        compiler_params=_params("parallel", "parallel"), name="neighbourhood_attention",
    )(za, za, za, za, za, za, za, bias)


def _split_dot(a_bf, x, dims):
    hi = x.astype(BF16)
    lo = (x - hi.astype(F32)).astype(BF16)
    return (lax.dot_general(a_bf, hi, dims, preferred_element_type=F32)
            + lax.dot_general(a_bf, lo, dims, preferred_element_type=F32))


def _gla_body(qf_ref, kf_ref, vf_ref, gf_ref, qb_ref, kb_ref, vb_ref, gb_ref,
              of_ref, ob_ref, sf_ref, sb_ref):
    c = GLA_CHUNK
    blk = GLA_BLOCK
    grp = GLA_GROUP
    nchunk = blk // c

    @pl.when(pl.program_id(1) == 0)
    def _():
        sf_ref[...] = jnp.zeros_like(sf_ref)
        sb_ref[...] = jnp.zeros_like(sb_ref)

    row = lax.broadcasted_iota(jnp.int32, (blk, blk), 0)
    col = lax.broadcasted_iota(jnp.int32, (blk, blk), 1)
    same_chunk = (row // c) == (col // c)
    grow = lax.broadcasted_iota(jnp.int32, (grp, grp), 0)
    gcol = lax.broadcasted_iota(jnp.int32, (grp, grp), 1)
    same_chunk_g = (grow // c) == (gcol // c)
    mm = (((1,), (0,)), ((), ()))
    nt = (((1,), (1,)), ((), ()))
    tn = (((0,), (0,)), ((), ()))

    def prepare(q_ref, k_ref, v_ref, g_ref, reverse):
        keep_blk = same_chunk & ((row <= col) if reverse else (row >= col))
        keep_grp = same_chunk_g & ((grow <= gcol) if reverse else (grow >= gcol))
        g = g_ref[...]
        cum = _split_dot(keep_blk.astype(BF16), g, mm)
        edge = 0 if reverse else c - 1
        tot = cum.reshape(nchunk, c, GLA_KW)[:, edge:edge + 1, :]
        tot_rows = jnp.broadcast_to(tot, (nchunk, c, GLA_KW)).reshape(blk, GLA_KW)
        q = q_ref[...] * (GLA_DK ** -0.5)
        k = k_ref[...]
        q_t = (q * jnp.exp(cum)).astype(BF16)
        k_t = (k * jnp.exp(-cum)).astype(BF16)
        k_d = (k * jnp.exp(tot_rows - cum)).astype(BF16)
        v = v_ref[...].astype(BF16)
        intra = []
        for gi in range(blk // grp):
            rows = slice(gi * grp, (gi + 1) * grp)
            per_head = []
            for h in range(GLA_HEADS):
                kc = slice(h * GLA_DK, (h + 1) * GLA_DK)
                vc = slice(h * GLA_DV, (h + 1) * GLA_DV)
                a = lax.dot_general(q_t[rows, kc], k_t[rows, kc], nt, preferred_element_type=F32)
                a = jnp.where(keep_grp, a, 0.0).astype(BF16)
                per_head.append(jnp.dot(a, v[rows, vc], preferred_element_type=F32))
            intra.append(jnp.concatenate(per_head, axis=1))
        return jnp.concatenate(intra, axis=0), q_t, k_d, v, jnp.exp(tot)

    dirs = [(prepare(qf_ref, kf_ref, vf_ref, gf_ref, False), of_ref, sf_ref, False),
            (prepare(qb_ref, kb_ref, vb_ref, gb_ref, True), ob_ref, sb_ref, True)]
    states = [[s_ref[h] for h in range(GLA_HEADS)] for _, _, s_ref, _ in dirs]
    for step in range(nchunk):
        for d, ((intra, q_t, k_d, v, decay), o_ref, _, reverse) in enumerate(dirs):
            j = nchunk - 1 - step if reverse else step
            rows = slice(j * c, (j + 1) * c)
            outs = []
            for h in range(GLA_HEADS):
                kc = slice(h * GLA_DK, (h + 1) * GLA_DK)
                vc = slice(h * GLA_DV, (h + 1) * GLA_DV)
                st = states[d][h]
                outs.append(lax.dot_general(q_t[rows, kc], st.astype(BF16), nt, preferred_element_type=F32))
                ds = lax.dot_general(v[rows, vc], k_d[rows, kc], tn, preferred_element_type=F32)
                states[d][h] = st * decay[j, :, kc] + ds
            o_ref[rows, :] = intra[rows, :] + jnp.concatenate(outs, axis=1)
    for d, (_, _, s_ref, _) in enumerate(dirs):
        for h in range(GLA_HEADS):
            s_ref[h] = states[d][h]


def gla_bidirectional(zb, batch, n):
    t = zb.shape[0]
    nb = n // GLA_BLOCK
    assert n % GLA_BLOCK == 0

    def fwd(col):
        return lambda b, i: (b * nb + i, col)

    def bwd(col):
        return lambda b, i: (b * nb + nb - 1 - i, col)

    def kw(m):
        return pl.BlockSpec((GLA_BLOCK, GLA_KW), m)

    def vw(m):
        return pl.BlockSpec((GLA_BLOCK, GLA_VW), m)

    return pl.pallas_call(
        _gla_body, grid=(batch, nb),
        in_specs=[kw(fwd(0)), kw(fwd(1)), vw(fwd(1)), kw(fwd(6)),
                  kw(bwd(0)), kw(bwd(1)), vw(bwd(1)), kw(bwd(7))],
        out_specs=[vw(fwd(0)), vw(bwd(0))],
        out_shape=[jax.ShapeDtypeStruct((t, GLA_VW), F32)] * 2,
        scratch_shapes=[pltpu.VMEM((GLA_HEADS, GLA_DV, GLA_DK), F32)] * 2,
        compiler_params=_params("parallel", "arbitrary"), name="gla_bidirectional",
    )(zb, zb, zb, zb, zb, zb, zb, zb)


def _silu(x):
    return x * jax.nn.sigmoid(x)


def _out_ab_body(oa_ref, of_ref, ob_ref, r_ref, gn_ref, wa_ref, wb_ref, x_ref, o_ref):
    o = of_ref[...] + ob_ref[...]
    r = r_ref[...]
    parts = []
    for h in range(GLA_HEADS):
        vc = slice(h * GLA_DV, (h + 1) * GLA_DV)
        parts.append((_rms(o[:, vc], gn_ref[...]) * _silu(r[:, vc])).astype(BF16))
    o_b = jnp.concatenate(parts, axis=1)
    o_ref[...] = (jnp.dot(oa_ref[...], wa_ref[...], preferred_element_type=F32)
                  + jnp.dot(o_b, wb_ref[...], preferred_element_type=F32) + x_ref[...])


def out_proj_ab(o_a, o_fw, o_bw, zb, gla_norm, w_out, x):
    t = x.shape[0]
    tm = ROW_TILE
    w = w_out.astype(BF16)
    return pl.pallas_call(
        _out_ab_body, grid=(t // tm,),
        in_specs=[_row_spec(tm, NA_WIDTH), _row_spec(tm, GLA_VW), _row_spec(tm, GLA_VW),
                  _row_spec(tm, GLA_VW, col=2), _full_spec((1, GLA_DV)),
                  _full_spec((NA_WIDTH, D_MODEL)), _full_spec((GLA_VW, D_MODEL)),
                  _row_spec(tm, D_MODEL)],
        out_specs=_row_spec(tm, D_MODEL),
        out_shape=jax.ShapeDtypeStruct((t, D_MODEL), F32),
        compiler_params=_params("parallel"), name="out_proj_ab",
    )(o_a, o_fw, o_bw, zb, gla_norm.reshape(1, -1), w[:NA_WIDTH], w[NA_WIDTH:], x)


def _in_c_body(x_ref, g_ref, wbg_ref, wcg_ref, wxt_ref, bg_ref, u_ref):
    hn = _rms(x_ref[...], g_ref[...]).astype(BF16)
    bg_ref[...] = jnp.dot(hn, wbg_ref[...], preferred_element_type=F32)
    u_ref[...] = (jnp.dot(hn, wcg_ref[...], preferred_element_type=F32)
                  * jnp.dot(hn, wxt_ref[...], preferred_element_type=F32))


def in_proj_c(x, gain, w_in):
    t = x.shape[0]
    tm = ROW_TILE
    w = w_in.astype(BF16)
    d = D_MODEL
    return pl.pallas_call(
        _in_c_body, grid=(t // tm,),
        in_specs=[_row_spec(tm, d), _full_spec((1, d))] + [_full_spec((d, d))] * 3,
        out_specs=[_row_spec(tm, d)] * 2,
        out_shape=[jax.ShapeDtypeStruct((t, d), F32)] * 2,
        compiler_params=_params("parallel"), name="in_proj_c",
    )(x, gain.reshape(1, -1), w[:, :d], w[:, d:2 * d], w[:, 2 * d:])


def _conv_out_body(u_ref, up_ref, un_ref, bg_ref, cw_ref, w_ref, x_ref, o_ref, *, tiles_per_seq):
    i = pl.program_id(0)
    tm = u_ref.shape[0]
    u = u_ref[...]
    row = lax.broadcasted_iota(jnp.int32, (tm, 1), 0)
    first = (i % tiles_per_seq) == 0
    last = (i % tiles_per_seq) == tiles_per_seq - 1
    prev_row = jnp.where(first, 0.0, up_ref[SUBLANES - 1:SUBLANES, :])
    next_row = jnp.where(last, 0.0, un_ref[0:1, :])
    u_prev = jnp.where(row == 0, prev_row, pltpu.roll(u, 1, axis=0))
    u_next = jnp.where(row == tm - 1, next_row, pltpu.roll(u, tm - 1, axis=0))
    conv = cw_ref[0:1, :] * u_prev + cw_ref[1:2, :] * u + cw_ref[2:3, :] * u_next
    o_ref[...] = (jnp.dot((bg_ref[...] * conv).astype(BF16), w_ref[...], preferred_element_type=F32)
                  + x_ref[...])


def conv_out_proj(u, bg, conv_w, w_out, x, n):
    t = x.shape[0]
    tm = ROW_TILE
    d = D_MODEL
    per = tm // SUBLANES
    nhalo = t // SUBLANES
    assert n % tm == 0
    return pl.pallas_call(
        functools.partial(_conv_out_body, tiles_per_seq=n // tm), grid=(t // tm,),
        in_specs=[_row_spec(tm, d),
                  pl.BlockSpec((SUBLANES, d), lambda i: (jnp.maximum(i * per - 1, 0), 0)),
                  pl.BlockSpec((SUBLANES, d), lambda i: (jnp.minimum((i + 1) * per, nhalo - 1), 0)),
                  _row_spec(tm, d), _full_spec((CONV_W, d)), _full_spec((d, d)), _row_spec(tm, d)],
        out_specs=_row_spec(tm, d),
        out_shape=jax.ShapeDtypeStruct((t, d), F32),
        compiler_params=_params("parallel"), name="conv_out_proj",
    )(u, u, u, bg, conv_w, w_out.astype(BF16), x)


def _mem_body(x_ref, g_ref, wq_ref, k_ref, v_ref, wo_ref, o_ref):
    x = x_ref[...]
    hn = _rms(x, g_ref[...]).astype(BF16)
    q = (jnp.dot(hn, wq_ref[...], preferred_element_type=F32) * MEM_HEAD_DIM ** -0.5).astype(BF16)
    nt = (((1,), (1,)), ((), ()))
    parts = []
    for h in range(MEM_HEADS):
        hc = slice(h * MEM_HEAD_DIM, (h + 1) * MEM_HEAD_DIM)
        s = lax.dot_general(q[:, hc], k_ref[0, :, hc], nt, preferred_element_type=F32)
        p = jnp.exp(s - jnp.max(s, axis=-1, keepdims=True))
        l = jnp.sum(p, axis=-1, keepdims=True)
        o = jnp.dot(p.astype(BF16), v_ref[0, :, hc], preferred_element_type=F32) / l
        parts.append(o.astype(BF16))
    o_ref[...] = jnp.dot(jnp.concatenate(parts, axis=1), wo_ref[...], preferred_element_type=F32) + x


def memory_attention(x, mem, gain, wq, wk, wv, wo, n):
    t, d = x.shape
    b, nm, _ = mem.shape
    tm = ROW_TILE
    per = n // tm
    memf = mem.reshape(b * nm, d)
    k = matmul(memf, wk.astype(BF16), BF16).reshape(b, nm, d)
    v = matmul(memf, wv.astype(BF16), BF16).reshape(b, nm, d)
    kv_spec = pl.BlockSpec((1, nm, d), lambda i: (i // per, 0, 0))
    return pl.pallas_call(
        _mem_body, grid=(t // tm,),
        in_specs=[_row_spec(tm, d), _full_spec((1, d)), _full_spec((d, d)), kv_spec, kv_spec,
                  _full_spec((d, d))],
        out_specs=_row_spec(tm, d),
        out_shape=jax.ShapeDtypeStruct((t, d), F32),
        compiler_params=_params("parallel"), name="memory_attention",
    )(x, gain.reshape(1, -1), wq.astype(BF16), k, v, wo.astype(BF16))


def _router_body(x_ref, g_ref, r_ref, aff_ref):
    hn = _rms(x_ref[...], g_ref[...])
    r = r_ref[...]
    hn_hi, r_hi = hn.astype(BF16), r.astype(BF16)
    hn_lo, r_lo = (hn - hn_hi.astype(F32)).astype(BF16), (r - r_hi.astype(F32)).astype(BF16)
    nt = (((1,), (1,)), ((), ()))
    logits = (lax.dot_general(r_hi, hn_hi, nt, preferred_element_type=F32)
              + lax.dot_general(r_hi, hn_lo, nt, preferred_element_type=F32)
              + lax.dot_general(r_lo, hn_hi, nt, preferred_element_type=F32))
    p = jnp.exp(logits - jnp.max(logits, axis=0, keepdims=True))
    aff_ref[...] = p / jnp.sum(p, axis=0, keepdims=True)


def router_affinity(x, gain, router):
    t, d = x.shape
    tm = ROW_TILE
    return pl.pallas_call(
        _router_body, grid=(t // tm,),
        in_specs=[_row_spec(tm, d), _full_spec((1, d)), _full_spec((N_EXPERTS, d))],
        out_specs=pl.BlockSpec((N_EXPERTS, tm), lambda i: (0, i)),
        out_shape=jax.ShapeDtypeStruct((N_EXPERTS, t), F32),
        compiler_params=_params("parallel"), name="router_affinity",
    )(x, gain.reshape(1, -1), router.T)


def _select_body(aff_ref, slot_ref, base_ref, cnt_ref, *, cap):
    e, nc, _ = aff_ref.shape
    per = MOE_TILE // LANES
    bits = pltpu.bitcast(aff_ref[...], jnp.int32)

    def count(mask):
        s = jnp.sum(mask.astype(jnp.int32), axis=1, keepdims=True)
        return jnp.sum(s, axis=2, keepdims=True)

    def bisect(b, thr):
        cand = thr | (jnp.int32(1) << (30 - b))
        return jnp.where(count(bits >= cand) >= cap, cand, thr)

    thr = lax.fori_loop(0, 31, bisect, jnp.zeros((e, 1, 1), jnp.int32))
    gt = bits > thr
    eq = bits == thr
    need_eq = cap - count(gt)

    li = lax.broadcasted_iota(jnp.int32, (LANES, LANES), 0)
    lj = lax.broadcasted_iota(jnp.int32, (LANES, LANES), 1)
    incl = (li <= lj).astype(BF16)
    ci = lax.broadcasted_iota(jnp.int32, (nc, nc), 0)
    cj = lax.broadcasted_iota(jnp.int32, (nc, nc), 1)
    strict = (cj < ci).astype(BF16)
    same_tile = (cj // per == ci // per).astype(BF16)
    same_tile_before = ((cj // per == ci // per) & (cj < ci)).astype(BF16)
    tiles_before = ((cj // per < ci // per) & (cj % per == 0)).astype(BF16)

    def chunk_prefix(m):
        within = jnp.dot(m, incl, preferred_element_type=F32)
        tot = jnp.broadcast_to(within[:, LANES - 1:LANES], (nc, LANES)).astype(BF16)
        return within, tot

    for x in range(e):
        eq_x = eq[x].astype(BF16)
        within, tot = chunk_prefix(eq_x)
        rank_eq = within - eq_x.astype(F32) + jnp.dot(strict, tot, preferred_element_type=F32)
        sel = gt[x] | (eq[x] & (rank_eq < need_eq[x].astype(F32)))
        sel_b = sel.astype(BF16)
        within, tot = chunk_prefix(sel_b)
        tile_cnt = jnp.dot(same_tile, tot, preferred_element_type=F32).astype(jnp.int32)
        base = jnp.dot(tiles_before, tile_cnt.astype(BF16), preferred_element_type=F32)
        in_tile = jnp.dot(same_tile_before, tot, preferred_element_type=F32)
        pos = (base + in_tile + within - sel_b.astype(F32)).astype(jnp.int32)
        slot_ref[x] = jnp.where(sel, pos, NOT_SELECTED)
        base_ref[x] = base.astype(jnp.int32)
        cnt_ref[x] = tile_cnt


def select_slots(aff_t, cap):
    e, t = aff_t.shape
    nc = t // LANES
    per = MOE_TILE // LANES
    slot, base, cnt = pl.pallas_call(
        functools.partial(_select_body, cap=cap),
        out_shape=[jax.ShapeDtypeStruct((e, nc, LANES), jnp.int32)] * 3,
        compiler_params=pltpu.CompilerParams(vmem_limit_bytes=VMEM_LIMIT_BYTES), name="select_slots",
    )(aff_t.reshape(e, nc, LANES))
    return slot.reshape(e, t), base[:, ::per, 0].reshape(-1), cnt[:, ::per, 0].reshape(-1)


def _one_hot_t(slot_ref, start, width, lo=None, hi=None):
    tm = slot_ref.shape[1]
    w = lax.broadcasted_iota(jnp.int32, (width, tm), 0)
    rows = []
    for e in range(N_EXPERTS):
        s = slot_ref[e:e + 1, :]
        hit = (s - start[e]) == w
        if lo is not None:
            hit = hit & (s >= lo[e])
        if hi is not None:
            hit = hit & (s < hi[e])
        rows.append(hit)
    return jnp.concatenate(rows, axis=0)


def _tile_scalars(base_ref, cnt_ref, i, nt):
    lo = [base_ref[e * nt + i] for e in range(N_EXPERTS)]
    hi = [lo[e] + cnt_ref[e * nt + i] for e in range(N_EXPERTS)]
    start = [(l // SLAB_ALIGN) * SLAB_ALIGN for l in lo]
    return start, hi


def _rounds(start, hi):
    m = hi[0] - start[0]
    for e in range(1, N_EXPERTS):
        m = jnp.maximum(m, hi[e] - start[e])
    return (m + MOE_SLAB - 1) // MOE_SLAB


def _dispatch_body(base_ref, cnt_ref, x_ref, g_ref, slot_ref, aff_ref, xe_ref, ge_ref,
                   xbuf, gbuf, xover, gover, xcarry, gcarry, sem, osem, *, nt, cap):
    i = pl.program_id(0)
    cur = i % 2
    w_ = MOE_SLAB
    al = SLAB_ALIGN
    hn = _rms(x_ref[...], g_ref[...]).astype(BF16)
    start, hi = _tile_scalars(base_ref, cnt_ref, i, nt)
    nxt = [(h // al) * al for h in hi]

    @pl.when(i == 0)
    def _():
        xcarry[...] = jnp.zeros_like(xcarry)
        gcarry[...] = jnp.zeros_like(gcarry)

    def gather(hot):
        rows = jnp.dot(hot.astype(BF16), hn, preferred_element_type=F32)
        hot_f = hot.astype(F32)
        n = hot.shape[0] // N_EXPERTS
        gates = [jnp.sum(hot_f[e * n:(e + 1) * n] * aff_ref[e:e + 1, :], axis=1, keepdims=True)
                 for e in range(N_EXPERTS)]
        return rows, gates

    def copies(xsrc, gsrc, st, e, xs, gs):
        dst = pl.ds(pl.multiple_of(st[e], al), w_)
        return (pltpu.make_async_copy(xsrc.at[e], xe_ref.at[e, dst], xs),
                pltpu.make_async_copy(gsrc.at[e], ge_ref.at[e, dst], gs))

    rows, gates = gather(_one_hot_t(slot_ref, start, w_))
    keep = []
    for e in range(N_EXPERTS):
        xc = xcarry[e].astype(F32)
        gc = gcarry[e]
        win = rows[e * w_:(e + 1) * w_]
        xbuf[cur, e] = jnp.concatenate([win[:al] + xc, win[al:]], axis=0).astype(BF16)
        gwin = jnp.broadcast_to(gates[e], (w_, LANES))
        gbuf[cur, e] = jnp.concatenate([gwin[:al] + gc, gwin[al:]], axis=0)
        keep.append((xc, gc))
    rows, gates = gather(_one_hot_t(slot_ref, nxt, al))
    for e in range(N_EXPERTS):
        same = (nxt[e] == start[e]).astype(F32)
        xcarry[e] = (rows[e * al:(e + 1) * al] + same * keep[e][0]).astype(BF16)
        gcarry[e] = jnp.broadcast_to(gates[e], (al, LANES)) + same * keep[e][1]

    @pl.when(i > 0)
    def _():
        st_p, _ = _tile_scalars(base_ref, cnt_ref, i - 1, nt)
        for e in range(N_EXPERTS):
            for cp in copies(xbuf.at[1 - cur], gbuf.at[1 - cur], st_p, e, sem.at[1 - cur, 0], sem.at[1 - cur, 1]):
                cp.wait()

    for e in range(N_EXPERTS):
        for cp in copies(xbuf.at[cur], gbuf.at[cur], start, e, sem.at[cur, 0], sem.at[cur, 1]):
            cp.start()

    def extra_round(k, carry):
        st_k = [s + k * w_ for s in start]
        rows, gates = gather(_one_hot_t(slot_ref, st_k, w_))
        for e in range(N_EXPERTS):
            xover[e] = rows[e * w_:(e + 1) * w_].astype(BF16)
            gover[e] = jnp.broadcast_to(gates[e], (w_, LANES))
        for e in range(N_EXPERTS):
            @pl.when(hi[e] - start[e] > k * w_)
            def _():
                for cp in copies(xover, gover, st_k, e, osem.at[0], osem.at[1]):
                    cp.start()
        for e in range(N_EXPERTS):
            @pl.when(hi[e] - start[e] > k * w_)
            def _():
                for cp in copies(xover, gover, st_k, e, osem.at[0], osem.at[1]):
                    cp.wait()
        return carry

    lax.fori_loop(1, _rounds(start, hi), extra_round, 0)

    @pl.when(i == nt - 1)
    def _():
        for e in range(N_EXPERTS):
            for cp in copies(xbuf.at[cur], gbuf.at[cur], start, e, sem.at[cur, 0], sem.at[cur, 1]):
                cp.wait()
        for e in range(N_EXPERTS):
            xover[e] = jnp.zeros((w_, xover.shape[-1]), BF16)
            gover[e] = jnp.zeros((w_, LANES), F32)
        tail = [cap] * N_EXPERTS
        for e in range(N_EXPERTS):
            for cp in copies(xover, gover, tail, e, osem.at[0], osem.at[1]):
                cp.start()
        for e in range(N_EXPERTS):
            for cp in copies(xover, gover, tail, e, osem.at[0], osem.at[1]):
                cp.wait()


def dispatch(x, gain, slot, aff_t, base, cnt, cap):
    t, d = x.shape
    tm = MOE_TILE
    nt = t // tm
    rows = cap + MOE_SLAB
    any_spec = pl.BlockSpec(memory_space=pl.ANY)
    grid_spec = pltpu.PrefetchScalarGridSpec(
        num_scalar_prefetch=2, grid=(nt,),
        in_specs=[pl.BlockSpec((tm, d), lambda i, *_: (i, 0)), pl.BlockSpec((1, d), lambda i, *_: (0, 0)),
                  pl.BlockSpec((N_EXPERTS, tm), lambda i, *_: (0, i)),
                  pl.BlockSpec((N_EXPERTS, tm), lambda i, *_: (0, i))],
        out_specs=[any_spec, any_spec],
        scratch_shapes=[pltpu.VMEM((2, N_EXPERTS, MOE_SLAB, d), BF16),
                        pltpu.VMEM((2, N_EXPERTS, MOE_SLAB, LANES), F32),
                        pltpu.VMEM((N_EXPERTS, MOE_SLAB, d), BF16),
                        pltpu.VMEM((N_EXPERTS, MOE_SLAB, LANES), F32),
                        pltpu.VMEM((N_EXPERTS, SLAB_ALIGN, d), BF16),
                        pltpu.VMEM((N_EXPERTS, SLAB_ALIGN, LANES), F32),
                        pltpu.SemaphoreType.DMA((2, 2)), pltpu.SemaphoreType.DMA((2,))])
    return pl.pallas_call(
        functools.partial(_dispatch_body, nt=nt, cap=cap), grid_spec=grid_spec,
        out_shape=[jax.ShapeDtypeStruct((N_EXPERTS, rows, d), BF16),
                   jax.ShapeDtypeStruct((N_EXPERTS, rows, LANES), F32)],
        compiler_params=_params("arbitrary"), name="moe_dispatch",
    )(base, cnt, x, gain.reshape(1, -1), slot, aff_t)


def _ffn_body(xe_ref, gate_ref, wg_ref, wu_ref, wd_ref, o_ref, wg_s, wu_s, wd_s):
    e = pl.program_id(0)
    j = pl.program_id(1)
    rg = wg_ref.shape[2]
    rd = wd_ref.shape[2]

    @pl.when(e < N_EXPERTS)
    def _():
        nxt = e % 2
        wg_s[nxt, pl.ds(pl.multiple_of(j * rg, rg), rg), :] = wg_ref[0, 0].astype(BF16)
        wu_s[nxt, pl.ds(pl.multiple_of(j * rg, rg), rg), :] = wu_ref[0, 0].astype(BF16)
        wd_s[nxt, pl.ds(pl.multiple_of(j * rd, rd), rd), :] = wd_ref[0, 0].astype(BF16)

    @pl.when(e > 0)
    def _():
        cur = (e - 1) % 2
        xe = xe_ref[0]
        a = jnp.dot(xe, wg_s[cur], preferred_element_type=F32)
        b = jnp.dot(xe, wu_s[cur], preferred_element_type=F32)
        hid = (_silu(a) * b).astype(BF16)
        o_ref[0] = (jnp.dot(hid, wd_s[cur], preferred_element_type=F32) * gate_ref[0][:, :1]).astype(o_ref.dtype)


def expert_ffn(xe, gate, wg, wu, wd, layer, cap):
    e, _, d = xe.shape
    f = wg.shape[-1]
    tm = FFN_TILE
    nj = cap // tm
    last = e - 1

    def rows(i, j):
        return (jnp.maximum(i - 1, 0), jnp.where(i == 0, 0, j), 0)

    def chunk(i, j):
        return (layer, jnp.minimum(i, last), jnp.where(i <= last, j, nj - 1), 0)

    return pl.pallas_call(
        _ffn_body, grid=(e + 1, nj),
        in_specs=[
            pl.BlockSpec((1, tm, d), rows),
            pl.BlockSpec((1, tm, LANES), rows),
            pl.BlockSpec((1, 1, d // nj, f), chunk),
            pl.BlockSpec((1, 1, d // nj, f), chunk),
            pl.BlockSpec((1, 1, f // nj, d), chunk),
        ],
        out_specs=pl.BlockSpec((1, tm, d), rows),
        out_shape=jax.ShapeDtypeStruct((e, cap, d), BF16),
        scratch_shapes=[pltpu.VMEM((2, d, f), BF16), pltpu.VMEM((2, d, f), BF16), pltpu.VMEM((2, f, d), BF16)],
        compiler_params=_params("arbitrary", "arbitrary"), name="expert_ffn",
    )(xe, gate, wg, wu, wd)


def _combine_body(base_ref, cnt_ref, x_ref, slot_ref, y_ref, g_ref, o_ref, ybuf, yover, sem, osem,
                  *, nt, cap, out_norm):
    i = pl.program_id(0)
    cur = i % 2
    w_ = MOE_SLAB
    tn = (((0,), (0,)), ((), ()))

    def window(st):
        return [pl.multiple_of(jnp.minimum(s, cap - w_), SLAB_ALIGN) for s in st]

    def fetch(step, buf, xs):
        st, _ = _tile_scalars(base_ref, cnt_ref, step, nt)
        ws = window(st)
        return [pltpu.make_async_copy(y_ref.at[e, pl.ds(ws[e], w_)], buf.at[e], xs)
                for e in range(N_EXPERTS)]

    @pl.when(i == 0)
    def _():
        yover[...] = jnp.zeros_like(yover)
        for cp in fetch(0, ybuf.at[0], sem.at[0]):
            cp.start()

    @pl.when(i + 1 < nt)
    def _():
        for cp in fetch(i + 1, ybuf.at[1 - cur], sem.at[1 - cur]):
            cp.start()

    for cp in fetch(i, ybuf.at[cur], sem.at[cur]):
        cp.wait()

    start, hi = _tile_scalars(base_ref, cnt_ref, i, nt)

    def scatter(buf, st_k, first):
        hot = _one_hot_t(slot_ref, window(st_k), w_, lo=None if first else st_k, hi=[s + w_ for s in st_k])
        ys = buf[...].reshape(N_EXPERTS * w_, buf.shape[-1])
        return lax.dot_general(hot.astype(BF16), ys, tn, preferred_element_type=F32)

    o_ref[...] = x_ref[...] + scatter(ybuf.at[cur], start, True)

    def extra_round(k, carry):
        st_k = [s + k * w_ for s in start]
        ws = window(st_k)
        for e in range(N_EXPERTS):
            @pl.when(hi[e] - start[e] > k * w_)
            def _():
                pltpu.make_async_copy(y_ref.at[e, pl.ds(ws[e], w_)], yover.at[e], osem).start()
        for e in range(N_EXPERTS):
            @pl.when(hi[e] - start[e] > k * w_)
            def _():
                pltpu.make_async_copy(y_ref.at[e, pl.ds(ws[e], w_)], yover.at[e], osem).wait()
        o_ref[...] += scatter(yover, st_k, False)
        return carry

    lax.fori_loop(1, _rounds(start, hi), extra_round, 0)
    if out_norm:
        o_ref[...] = _rms(o_ref[...], g_ref[...])


def combine(x, slot, y, base, cnt, cap, out_gain, out_norm):
    t, d = x.shape
    tm = MOE_TILE
    nt = t // tm
    grid_spec = pltpu.PrefetchScalarGridSpec(
        num_scalar_prefetch=2, grid=(nt,),
        in_specs=[pl.BlockSpec((tm, d), lambda i, *_: (i, 0)),
                  pl.BlockSpec((N_EXPERTS, tm), lambda i, *_: (0, i)),
                  pl.BlockSpec(memory_space=pl.ANY),
                  pl.BlockSpec((1, d), lambda i, *_: (0, 0))],
        out_specs=pl.BlockSpec((tm, d), lambda i, *_: (i, 0)),
        scratch_shapes=[pltpu.VMEM((2, N_EXPERTS, MOE_SLAB, d), BF16),
                        pltpu.VMEM((N_EXPERTS, MOE_SLAB, d), BF16),
                        pltpu.SemaphoreType.DMA((2,)), pltpu.SemaphoreType.DMA(())])
    return pl.pallas_call(
        functools.partial(_combine_body, nt=nt, cap=cap, out_norm=out_norm), grid_spec=grid_spec,
        out_shape=jax.ShapeDtypeStruct((t, d), F32),
        compiler_params=_params("arbitrary"), name="moe_combine",
    )(base, cnt, x, slot, y, out_gain.reshape(1, -1))


def expert_choice_ffn(x, layer, p):
    t, _ = x.shape
    gain = p['norm_ffn'][layer]
    cap = EC_CAPACITY * t // N_EXPERTS
    aff_t = router_affinity(x, gain, p['router'][layer])
    slot, base, cnt = select_slots(aff_t, cap)
    xe, ge = dispatch(x, gain, slot, aff_t, base, cnt, cap)
    y = expert_ffn(xe, ge, p['w_gate'], p['w_up'], p['w_down'], layer, cap)
    return combine(x, slot, y, base, cnt, cap, p['final_norm'], layer == DEPTH - 1)


def trunk(x, mem, p):
    b, n, d = x.shape
    x = x.reshape(b * n, d)
    for layer in range(DEPTH):
        i = layer // 2
        if layer % 2 == 0:
            za, zb = in_proj_ab(x, p['norm_mix'][layer], p['w_in_ab'][i], p['gla_gw_f'][i],
                                p['gla_gb_f'][i], p['gla_gw_b'][i], p['gla_gb_b'][i])
            o_a = neighbourhood_attention(za, p['na_rpb'][i], b, n)
            o_fw, o_bw = gla_bidirectional(zb, b, n)
            x = out_proj_ab(o_a, o_fw, o_bw, zb, p['gla_norm'][i], p['w_out_ab'][i], x)
        else:
            bg, u = in_proj_c(x, p['norm_mix'][layer], p['w_in_c'][i])
            x = conv_out_proj(u, bg, p['conv_w'][i], p['w_out_c'][i], x, n)
        x = memory_attention(x, mem, p['norm_mem'][layer], p['w_mq'][layer], p['w_mk'][layer],
                             p['w_mv'][layer], p['w_mo'][layer], n)
        x = expert_choice_ffn(x, layer, p)
    return x.reshape(b, n, d)


def kernel(x_prompt, x_sample, mem_prompt, mem_sample, w_in_ab, na_rpb, gla_gw_f, gla_gb_f, gla_gw_b,
           gla_gb_b, gla_norm, w_out_ab, w_in_c, conv_w, w_out_c, norm_mix, norm_mem, norm_ffn,
           w_mq, w_mk, w_mv, w_mo, router, w_gate, w_up, w_down, final_norm):
    p = dict(w_in_ab=w_in_ab, na_rpb=na_rpb, gla_gw_f=gla_gw_f, gla_gb_f=gla_gb_f, gla_gw_b=gla_gw_b,
             gla_gb_b=gla_gb_b, gla_norm=gla_norm, w_out_ab=w_out_ab, w_in_c=w_in_c, conv_w=conv_w,
             w_out_c=w_out_c, norm_mix=norm_mix, norm_mem=norm_mem, norm_ffn=norm_ffn, w_mq=w_mq,
             w_mk=w_mk, w_mv=w_mv, w_mo=w_mo, router=router, w_gate=w_gate, w_up=w_up, w_down=w_down,
             final_norm=final_norm)
    y_prompt = trunk(x_prompt, mem_prompt, p)
    y_sample = trunk(x_sample, mem_sample, p)
    return (y_prompt, y_sample)
```

```python
import functools

import jax
import jax.numpy as jnp
import numpy as np
from jax import lax
from jax.experimental import pallas as pl
from jax.experimental.pallas import tpu as pltpu

D_MODEL = 1024
DEPTH = 4
GRID_W = 64
EPS = 1e-6
NEG_INF = -1e30
NA_HEADS = 8
NA_HEAD_DIM = 64
NA_WIN_R = 8
NA_WIN_C = 16
NA_WIDTH = NA_HEADS * NA_HEAD_DIM
GLA_HEADS = 4
GLA_DK = 64
GLA_DV = 128
GLA_RANK = 16
GLA_TAU = 16.0
GLA_CHUNK = 64
GLA_KW = GLA_HEADS * GLA_DK
GLA_VW = GLA_HEADS * GLA_DV
CONV_W = 3
MEM_HEADS = 4
MEM_HEAD_DIM = D_MODEL // MEM_HEADS
N_EXPERTS = 16
D_EXPERT = 2 * D_MODEL
EC_CAPACITY = 2

VMEM_LIMIT_BYTES = 48 * 1024 * 1024
ROW_TILE = 1024
SUBLANES = 8
NA_ROWS_PER_BLOCK = 4
NA_BLOCK_TOKENS = NA_ROWS_PER_BLOCK * GRID_W
GLA_BLOCK = 512
GLA_GROUP = 256
LANES = 128
NOT_SELECTED = -(1 << 24)
MOE_TILE = 256
MOE_SLAB = 64
SLAB_ALIGN = 16
FFN_TILE = 512
BF16 = jnp.bfloat16
F32 = jnp.float32


def _params(*sem):
    return pltpu.CompilerParams(dimension_semantics=sem, vmem_limit_bytes=VMEM_LIMIT_BYTES)


def _rms(x, g):
    return x * lax.rsqrt(jnp.mean(x * x, axis=-1, keepdims=True) + EPS) * g


def _row_spec(tm, n, col=0):
    return pl.BlockSpec((tm, n), lambda i: (i, col))


def _full_spec(shape):
    return pl.BlockSpec(shape, lambda *_: (0,) * len(shape))


def _mm_body(x_ref, w_ref, o_ref):
    o_ref[...] = jnp.dot(x_ref[...].astype(BF16), w_ref[...],
                         preferred_element_type=F32).astype(o_ref.dtype)


def matmul(x, w, out_dtype):
    m, k = x.shape
    n = w.shape[1]
    tm = min(ROW_TILE, m)
    return pl.pallas_call(
        _mm_body, grid=(m // tm,),
        in_specs=[_row_spec(tm, k), _full_spec((k, n))],
        out_specs=_row_spec(tm, n),
        out_shape=jax.ShapeDtypeStruct((m, n), out_dtype),
        compiler_params=_params("parallel"), name="row_matmul",
    )(x, w)


def _log_sigmoid(x):
    return jnp.minimum(x, 0.0) - jnp.log1p(jnp.exp(-jnp.abs(x)))


def _in_ab_body(x_ref, g_ref, wa_ref, wb_ref, wg_ref, gw_ref, gb_ref, za_ref, zb_ref):
    hn = _rms(x_ref[...], g_ref[...]).astype(BF16)
    za_ref[...] = jnp.dot(hn, wa_ref[...], preferred_element_type=F32).astype(za_ref.dtype)
    nb = wb_ref.shape[1]
    zb_ref[:, :nb] = jnp.dot(hn, wb_ref[...], preferred_element_type=F32)
    lowrank = jnp.dot(hn, wg_ref[...], preferred_element_type=F32)
    hi = lowrank.astype(BF16)
    lo = (lowrank - hi.astype(F32)).astype(BF16)
    pre = jnp.dot(jnp.concatenate([hi, hi, lo], axis=1), gw_ref[...], preferred_element_type=F32) + gb_ref[...]
    zb_ref[:, nb:] = _log_sigmoid(pre) / GLA_TAU


def in_proj_ab(x, gain, w_in, gwf, gbf, gwb, gbb):
    t = x.shape[0]
    tm = ROW_TILE
    na_w = 3 * NA_WIDTH
    gl_w = 2 * GLA_KW + 2 * GLA_VW
    o = np.cumsum([0, NA_WIDTH, NA_WIDTH, NA_WIDTH, GLA_KW, GLA_KW, GLA_VW, GLA_RANK, GLA_RANK, GLA_VW])
    wa = w_in[:, :o[3]].astype(BF16)
    wb = jnp.concatenate([w_in[:, o[3]:o[6]], w_in[:, o[8]:o[9]]], axis=1).astype(BF16)
    wg = w_in[:, o[6]:o[8]].astype(BF16)
    zero = jnp.zeros_like(gwf)
    gw = jnp.concatenate([jnp.concatenate([gwf, zero], axis=1), jnp.concatenate([zero, gwb], axis=1)], axis=0)
    gw_hi = gw.astype(BF16)
    gw_lo = (gw - gw_hi.astype(F32)).astype(BF16)
    gw3 = jnp.concatenate([gw_hi, gw_lo, gw_hi], axis=0)
    gbias = jnp.concatenate([gbf, gbb]).reshape(1, -1)
    return pl.pallas_call(
        _in_ab_body, grid=(t // tm,),
        in_specs=[_row_spec(tm, D_MODEL), _full_spec((1, D_MODEL)), _full_spec((D_MODEL, na_w)),
                  _full_spec((D_MODEL, gl_w)), _full_spec((D_MODEL, 2 * GLA_RANK)),
                  _full_spec((6 * GLA_RANK, 2 * GLA_KW)), _full_spec((1, 2 * GLA_KW))],
        out_specs=[_row_spec(tm, na_w), _row_spec(tm, gl_w + 2 * GLA_KW)],
        out_shape=[jax.ShapeDtypeStruct((t, na_w), BF16),
                   jax.ShapeDtypeStruct((t, gl_w + 2 * GLA_KW), F32)],
        compiler_params=_params("parallel"), name="in_proj_ab",
    )(x, gain.reshape(1, -1), wa, wb, wg, gw3, gbias)


def na_bias_table(rpb):
    rb = NA_ROWS_PER_BLOCK
    a = np.arange(rb)[:, None, None, None]
    c = np.arange(GRID_W)[None, :, None, None]
    u = np.arange(3 * rb)[None, None, :, None]
    kc = np.arange(GRID_W)[None, None, None, :]
    wstart = np.clip(c - NA_WIN_C // 2, 0, GRID_W - NA_WIN_C)
    col_ok = (kc >= wstart) & (kc < wstart + NA_WIN_C)
    dc = np.clip(kc - c + NA_WIN_C - 1, 0, 2 * NA_WIN_C - 2)
    dr = np.clip(u - a + NA_WIN_R // 2 - 1, 0, 2 * NA_WIN_R - 2)
    row_ok = [
        (u >= rb) & (u < rb + NA_WIN_R) & (a >= 0),
        (u - a >= 0) & (u - a < NA_WIN_R),
        (u >= 0) & (u < NA_WIN_R) & (a >= 0),
    ]
    shape = (rb, GRID_W, 3 * rb, GRID_W)
    flat = (NA_BLOCK_TOKENS, 3 * NA_BLOCK_TOKENS)
    pick_r = jnp.asarray(np.eye(2 * NA_WIN_R - 1, dtype=np.float32)[dr[:, 0, :, 0]])
    pick_c = jnp.asarray(np.eye(2 * NA_WIN_C - 1, dtype=np.float32)[dc[0, :, 0, :]])
    vals = jnp.einsum('aud,hdp,ckp->hacuk', pick_r, rpb.astype(F32), pick_c,
                      precision=lax.Precision.HIGHEST).reshape((NA_HEADS,) + flat)
    out = []
    for ok in row_ok:
        m = np.broadcast_to(ok & col_ok, shape).reshape(flat)
        out.append(jnp.where(jnp.asarray(m)[None], vals, NEG_INF))
    return jnp.stack(out)


def _na_body(q_ref, kp_ref, kc_ref, kn_ref, vp_ref, vc_ref, vn_ref, bias_ref, o_ref):
    lane = lax.broadcasted_iota(jnp.int32, (1, 2 * NA_HEAD_DIM), 1)
    scale = NA_HEAD_DIM ** -0.5
    nt = (((1,), (1,)), ((), ()))
    for hp in range(NA_HEADS // 2):
        cols = slice(hp * 2 * NA_HEAD_DIM, (hp + 1) * 2 * NA_HEAD_DIM)
        q = q_ref[:, cols] * scale
        ks = [r[:, cols] for r in (kp_ref, kc_ref, kn_ref)]
        vs = [r[:, cols] for r in (vp_ref, vc_ref, vn_ref)]
        o_pair = None
        for sub in range(2):
            sel = (lane < NA_HEAD_DIM) if sub == 0 else (lane >= NA_HEAD_DIM)
            qm = jnp.where(sel, q, jnp.zeros_like(q))
            s = jnp.concatenate(
                [lax.dot_general(qm, kk, nt, preferred_element_type=F32) for kk in ks], axis=1)
            s = s + bias_ref[0, 2 * hp + sub]
            p = jnp.exp(s - jnp.max(s, axis=-1, keepdims=True))
            l = jnp.sum(p, axis=-1, keepdims=True)
            pb = p.astype(BF16)
            o = None
            for j, vv in enumerate(vs):
                t = jnp.dot(pb[:, j * NA_BLOCK_TOKENS:(j + 1) * NA_BLOCK_TOKENS], vv,
                            preferred_element_type=F32)
                o = t if o is None else o + t
            o = o / l
            o_pair = o if o_pair is None else jnp.where(sel, o, o_pair)
        o_ref[:, cols] = o_pair.astype(o_ref.dtype)


def neighbourhood_attention(za, rpb, batch, n):
    t = za.shape[0]
    bt = NA_BLOCK_TOKENS
    nblk = n // bt
    assert n % bt == 0 and nblk >= 3
    bias = na_bias_table(rpb)

    def qmap(b, i):
        return (b * nblk + i, 0)

    def kmap(d, col):
        return lambda b, i: (b * nblk + jnp.clip(i + d, 0, nblk - 1), col)

    def bmap(b, i):
        return (jnp.where(i == 0, 0, jnp.where(i == nblk - 1, 2, 1)), 0, 0, 0)

    def blk(m):
        return pl.BlockSpec((bt, NA_WIDTH), m)

    return pl.pallas_call(
        _na_body, grid=(batch, nblk),
        in_specs=[blk(qmap), blk(kmap(-1, 1)), blk(kmap(0, 1)), blk(kmap(1, 1)),
                  blk(kmap(-1, 2)), blk(kmap(0, 2)), blk(kmap(1, 2)),
                  pl.BlockSpec((1, NA_HEADS, bt, 3 * bt), bmap)],
        out_specs=blk(qmap),
        out_shape=jax.ShapeDtypeStruct((t, NA_WIDTH), BF16),
        compiler_params=_params("parallel", "parallel"), name="neighbourhood_attention",
    )(za, za, za, za, za, za, za, bias)


def _split_dot(a_bf, x, dims):
    hi = x.astype(BF16)
    lo = (x - hi.astype(F32)).astype(BF16)
    return (lax.dot_general(a_bf, hi, dims, preferred_element_type=F32)
            + lax.dot_general(a_bf, lo, dims, preferred_element_type=F32))


def _gla_body(qf_ref, kf_ref, vf_ref, gf_ref, qb_ref, kb_ref, vb_ref, gb_ref,
              of_ref, ob_ref, sf_ref, sb_ref):
    c = GLA_CHUNK
    blk = GLA_BLOCK
    grp = GLA_GROUP
    nchunk = blk // c

    @pl.when(pl.program_id(1) == 0)
    def _():
        sf_ref[...] = jnp.zeros_like(sf_ref)
        sb_ref[...] = jnp.zeros_like(sb_ref)

    row = lax.broadcasted_iota(jnp.int32, (blk, blk), 0)
    col = lax.broadcasted_iota(jnp.int32, (blk, blk), 1)
    same_chunk = (row // c) == (col // c)
    grow = lax.broadcasted_iota(jnp.int32, (grp, grp), 0)
    gcol = lax.broadcasted_iota(jnp.int32, (grp, grp), 1)
    same_chunk_g = (grow // c) == (gcol // c)
    mm = (((1,), (0,)), ((), ()))
    nt = (((1,), (1,)), ((), ()))
    tn = (((0,), (0,)), ((), ()))

    def prepare(q_ref, k_ref, v_ref, g_ref, reverse):
        keep_blk = same_chunk & ((row <= col) if reverse else (row >= col))
        keep_grp = same_chunk_g & ((grow <= gcol) if reverse else (grow >= gcol))
        g = g_ref[...]
        cum = _split_dot(keep_blk.astype(BF16), g, mm)
        edge = 0 if reverse else c - 1
        tot = cum.reshape(nchunk, c, GLA_KW)[:, edge:edge + 1, :]
        tot_rows = jnp.broadcast_to(tot, (nchunk, c, GLA_KW)).reshape(blk, GLA_KW)
        q = q_ref[...] * (GLA_DK ** -0.5)
        k = k_ref[...]
        q_t = (q * jnp.exp(cum)).astype(BF16)
        k_t = (k * jnp.exp(-cum)).astype(BF16)
        k_d = (k * jnp.exp(tot_rows - cum)).astype(BF16)
        v = v_ref[...].astype(BF16)
        intra = []
        for gi in range(blk // grp):
            rows = slice(gi * grp, (gi + 1) * grp)
            per_head = []
            for h in range(GLA_HEADS):
                kc = slice(h * GLA_DK, (h + 1) * GLA_DK)
                vc = slice(h * GLA_DV, (h + 1) * GLA_DV)
                a = lax.dot_general(q_t[rows, kc], k_t[rows, kc], nt, preferred_element_type=F32)
                a = jnp.where(keep_grp, a, 0.0).astype(BF16)
                per_head.append(jnp.dot(a, v[rows, vc], preferred_element_type=F32))
            intra.append(jnp.concatenate(per_head, axis=1))
        return jnp.concatenate(intra, axis=0), q_t, k_d, v, jnp.exp(tot)

    dirs = [(prepare(qf_ref, kf_ref, vf_ref, gf_ref, False), of_ref, sf_ref, False),
            (prepare(qb_ref, kb_ref, vb_ref, gb_ref, True), ob_ref, sb_ref, True)]
    states = [[s_ref[h] for h in range(GLA_HEADS)] for _, _, s_ref, _ in dirs]
    for step in range(nchunk):
        for d, ((intra, q_t, k_d, v, decay), o_ref, _, reverse) in enumerate(dirs):
            j = nchunk - 1 - step if reverse else step
            rows = slice(j * c, (j + 1) * c)
            outs = []
            for h in range(GLA_HEADS):
                kc = slice(h * GLA_DK, (h + 1) * GLA_DK)
                vc = slice(h * GLA_DV, (h + 1) * GLA_DV)
                st = states[d][h]
                outs.append(lax.dot_general(q_t[rows, kc], st.astype(BF16), nt, preferred_element_type=F32))
                ds = lax.dot_general(v[rows, vc], k_d[rows, kc], tn, preferred_element_type=F32)
                states[d][h] = st * decay[j, :, kc] + ds
            o_ref[rows, :] = intra[rows, :] + jnp.concatenate(outs, axis=1)
    for d, (_, _, s_ref, _) in enumerate(dirs):
        for h in range(GLA_HEADS):
            s_ref[h] = states[d][h]


def gla_bidirectional(zb, batch, n):
    t = zb.shape[0]
    nb = n // GLA_BLOCK
    assert n % GLA_BLOCK == 0

    def fwd(col):
        return lambda b, i: (b * nb + i, col)

    def bwd(col):
        return lambda b, i: (b * nb + nb - 1 - i, col)

    def kw(m):
        return pl.BlockSpec((GLA_BLOCK, GLA_KW), m)

    def vw(m):
        return pl.BlockSpec((GLA_BLOCK, GLA_VW), m)

    return pl.pallas_call(
        _gla_body, grid=(batch, nb),
        in_specs=[kw(fwd(0)), kw(fwd(1)), vw(fwd(1)), kw(fwd(6)),
                  kw(bwd(0)), kw(bwd(1)), vw(bwd(1)), kw(bwd(7))],
        out_specs=[vw(fwd(0)), vw(bwd(0))],
        out_shape=[jax.ShapeDtypeStruct((t, GLA_VW), F32)] * 2,
        scratch_shapes=[pltpu.VMEM((GLA_HEADS, GLA_DV, GLA_DK), F32)] * 2,
        compiler_params=_params("parallel", "arbitrary"), name="gla_bidirectional",
    )(zb, zb, zb, zb, zb, zb, zb, zb)


def _silu(x):
    return x * jax.nn.sigmoid(x)


def _out_ab_body(oa_ref, of_ref, ob_ref, r_ref, gn_ref, wa_ref, wb_ref, x_ref, o_ref):
    o = of_ref[...] + ob_ref[...]
    r = r_ref[...]
    parts = []
    for h in range(GLA_HEADS):
        vc = slice(h * GLA_DV, (h + 1) * GLA_DV)
        parts.append((_rms(o[:, vc], gn_ref[...]) * _silu(r[:, vc])).astype(BF16))
    o_b = jnp.concatenate(parts, axis=1)
    o_ref[...] = (jnp.dot(oa_ref[...], wa_ref[...], preferred_element_type=F32)
                  + jnp.dot(o_b, wb_ref[...], preferred_element_type=F32) + x_ref[...])


def out_proj_ab(o_a, o_fw, o_bw, zb, gla_norm, w_out, x):
    t = x.shape[0]
    tm = ROW_TILE
    w = w_out.astype(BF16)
    return pl.pallas_call(
        _out_ab_body, grid=(t // tm,),
        in_specs=[_row_spec(tm, NA_WIDTH), _row_spec(tm, GLA_VW), _row_spec(tm, GLA_VW),
                  _row_spec(tm, GLA_VW, col=2), _full_spec((1, GLA_DV)),
                  _full_spec((NA_WIDTH, D_MODEL)), _full_spec((GLA_VW, D_MODEL)),
                  _row_spec(tm, D_MODEL)],
        out_specs=_row_spec(tm, D_MODEL),
        out_shape=jax.ShapeDtypeStruct((t, D_MODEL), F32),
        compiler_params=_params("parallel"), name="out_proj_ab",
    )(o_a, o_fw, o_bw, zb, gla_norm.reshape(1, -1), w[:NA_WIDTH], w[NA_WIDTH:], x)


def _in_c_body(x_ref, g_ref, wbg_ref, wcg_ref, wxt_ref, bg_ref, u_ref):
    hn = _rms(x_ref[...], g_ref[...]).astype(BF16)
    bg_ref[...] = jnp.dot(hn, wbg_ref[...], preferred_element_type=F32)
    u_ref[...] = (jnp.dot(hn, wcg_ref[...], preferred_element_type=F32)
                  * jnp.dot(hn, wxt_ref[...], preferred_element_type=F32))


def in_proj_c(x, gain, w_in):
    t = x.shape[0]
    tm = ROW_TILE
    w = w_in.astype(BF16)
    d = D_MODEL
    return pl.pallas_call(
        _in_c_body, grid=(t // tm,),
        in_specs=[_row_spec(tm, d), _full_spec((1, d))] + [_full_spec((d, d))] * 3,
        out_specs=[_row_spec(tm, d)] * 2,
        out_shape=[jax.ShapeDtypeStruct((t, d), F32)] * 2,
        compiler_params=_params("parallel"), name="in_proj_c",
    )(x, gain.reshape(1, -1), w[:, :d], w[:, d:2 * d], w[:, 2 * d:])


def _conv_out_body(u_ref, up_ref, un_ref, bg_ref, cw_ref, w_ref, x_ref, o_ref, *, tiles_per_seq):
    i = pl.program_id(0)
    tm = u_ref.shape[0]
    u = u_ref[...]
    row = lax.broadcasted_iota(jnp.int32, (tm, 1), 0)
    first = (i % tiles_per_seq) == 0
    last = (i % tiles_per_seq) == tiles_per_seq - 1
    prev_row = jnp.where(first, 0.0, up_ref[SUBLANES - 1:SUBLANES, :])
    next_row = jnp.where(last, 0.0, un_ref[0:1, :])
    u_prev = jnp.where(row == 0, prev_row, pltpu.roll(u, 1, axis=0))
    u_next = jnp.where(row == tm - 1, next_row, pltpu.roll(u, tm - 1, axis=0))
    conv = cw_ref[0:1, :] * u_prev + cw_ref[1:2, :] * u + cw_ref[2:3, :] * u_next
    o_ref[...] = (jnp.dot((bg_ref[...] * conv).astype(BF16), w_ref[...], preferred_element_type=F32)
                  + x_ref[...])


def conv_out_proj(u, bg, conv_w, w_out, x, n):
    t = x.shape[0]
    tm = ROW_TILE
    d = D_MODEL
    per = tm // SUBLANES
    nhalo = t // SUBLANES
    assert n % tm == 0
    return pl.pallas_call(
        functools.partial(_conv_out_body, tiles_per_seq=n // tm), grid=(t // tm,),
        in_specs=[_row_spec(tm, d),
                  pl.BlockSpec((SUBLANES, d), lambda i: (jnp.maximum(i * per - 1, 0), 0)),
                  pl.BlockSpec((SUBLANES, d), lambda i: (jnp.minimum((i + 1) * per, nhalo - 1), 0)),
                  _row_spec(tm, d), _full_spec((CONV_W, d)), _full_spec((d, d)), _row_spec(tm, d)],
        out_specs=_row_spec(tm, d),
        out_shape=jax.ShapeDtypeStruct((t, d), F32),
        compiler_params=_params("parallel"), name="conv_out_proj",
    )(u, u, u, bg, conv_w, w_out.astype(BF16), x)


def _mem_body(x_ref, g_ref, wq_ref, k_ref, v_ref, wo_ref, o_ref):
    x = x_ref[...]
    hn = _rms(x, g_ref[...]).astype(BF16)
    q = (jnp.dot(hn, wq_ref[...], preferred_element_type=F32) * MEM_HEAD_DIM ** -0.5).astype(BF16)
    nt = (((1,), (1,)), ((), ()))
    parts = []
    for h in range(MEM_HEADS):
        hc = slice(h * MEM_HEAD_DIM, (h + 1) * MEM_HEAD_DIM)
        s = lax.dot_general(q[:, hc], k_ref[0, :, hc], nt, preferred_element_type=F32)
        p = jnp.exp(s - jnp.max(s, axis=-1, keepdims=True))
        l = jnp.sum(p, axis=-1, keepdims=True)
        o = jnp.dot(p.astype(BF16), v_ref[0, :, hc], preferred_element_type=F32) / l
        parts.append(o.astype(BF16))
    o_ref[...] = jnp.dot(jnp.concatenate(parts, axis=1), wo_ref[...], preferred_element_type=F32) + x


def memory_attention(x, mem, gain, wq, wk, wv, wo, n):
    t, d = x.shape
    b, nm, _ = mem.shape
    tm = ROW_TILE
    per = n // tm
    memf = mem.reshape(b * nm, d)
    k = matmul(memf, wk.astype(BF16), BF16).reshape(b, nm, d)
    v = matmul(memf, wv.astype(BF16), BF16).reshape(b, nm, d)
    kv_spec = pl.BlockSpec((1, nm, d), lambda i: (i // per, 0, 0))
    return pl.pallas_call(
        _mem_body, grid=(t // tm,),
        in_specs=[_row_spec(tm, d), _full_spec((1, d)), _full_spec((d, d)), kv_spec, kv_spec,
                  _full_spec((d, d))],
        out_specs=_row_spec(tm, d),
        out_shape=jax.ShapeDtypeStruct((t, d), F32),
        compiler_params=_params("parallel"), name="memory_attention",
    )(x, gain.reshape(1, -1), wq.astype(BF16), k, v, wo.astype(BF16))


def _router_body(x_ref, g_ref, r_ref, aff_ref):
    hn = _rms(x_ref[...], g_ref[...])
    r = r_ref[...]
    hn_hi, r_hi = hn.astype(BF16), r.astype(BF16)
    hn_lo, r_lo = (hn - hn_hi.astype(F32)).astype(BF16), (r - r_hi.astype(F32)).astype(BF16)
    nt = (((1,), (1,)), ((), ()))
    logits = (lax.dot_general(r_hi, hn_hi, nt, preferred_element_type=F32)
              + lax.dot_general(r_hi, hn_lo, nt, preferred_element_type=F32)
              + lax.dot_general(r_lo, hn_hi, nt, preferred_element_type=F32))
    p = jnp.exp(logits - jnp.max(logits, axis=0, keepdims=True))
    aff_ref[...] = p / jnp.sum(p, axis=0, keepdims=True)


def router_affinity(x, gain, router):
    t, d = x.shape
    tm = ROW_TILE
    return pl.pallas_call(
        _router_body, grid=(t // tm,),
        in_specs=[_row_spec(tm, d), _full_spec((1, d)), _full_spec((N_EXPERTS, d))],
        out_specs=pl.BlockSpec((N_EXPERTS, tm), lambda i: (0, i)),
        out_shape=jax.ShapeDtypeStruct((N_EXPERTS, t), F32),
        compiler_params=_params("parallel"), name="router_affinity",
    )(x, gain.reshape(1, -1), router.T)


def _select_body(aff_ref, slot_ref, base_ref, cnt_ref, *, cap):
    e, nc, _ = aff_ref.shape
    per = MOE_TILE // LANES
    bits = pltpu.bitcast(aff_ref[...], jnp.int32)

    def count(mask):
        s = jnp.sum(mask.astype(jnp.int32), axis=1, keepdims=True)
        return jnp.sum(s, axis=2, keepdims=True)

    def bisect(b, thr):
        cand = thr | (jnp.int32(1) << (30 - b))
        return jnp.where(count(bits >= cand) >= cap, cand, thr)

    thr = lax.fori_loop(0, 31, bisect, jnp.zeros((e, 1, 1), jnp.int32))
    gt = bits > thr
    eq = bits == thr
    need_eq = cap - count(gt)

    li = lax.broadcasted_iota(jnp.int32, (LANES, LANES), 0)
    lj = lax.broadcasted_iota(jnp.int32, (LANES, LANES), 1)
    incl = (li <= lj).astype(BF16)
    ci = lax.broadcasted_iota(jnp.int32, (nc, nc), 0)
    cj = lax.broadcasted_iota(jnp.int32, (nc, nc), 1)
    strict = (cj < ci).astype(BF16)
    same_tile = (cj // per == ci // per).astype(BF16)
    same_tile_before = ((cj // per == ci // per) & (cj < ci)).astype(BF16)
    tiles_before = ((cj // per < ci // per) & (cj % per == 0)).astype(BF16)

    def chunk_prefix(m):
        within = jnp.dot(m, incl, preferred_element_type=F32)
        tot = jnp.broadcast_to(within[:, LANES - 1:LANES], (nc, LANES)).astype(BF16)
        return within, tot

    for x in range(e):
        eq_x = eq[x].astype(BF16)
        within, tot = chunk_prefix(eq_x)
        rank_eq = within - eq_x.astype(F32) + jnp.dot(strict, tot, preferred_element_type=F32)
        sel = gt[x] | (eq[x] & (rank_eq < need_eq[x].astype(F32)))
        sel_b = sel.astype(BF16)
        within, tot = chunk_prefix(sel_b)
        tile_cnt = jnp.dot(same_tile, tot, preferred_element_type=F32).astype(jnp.int32)
        base = jnp.dot(tiles_before, tile_cnt.astype(BF16), preferred_element_type=F32)
        in_tile = jnp.dot(same_tile_before, tot, preferred_element_type=F32)
        pos = (base + in_tile + within - sel_b.astype(F32)).astype(jnp.int32)
        slot_ref[x] = jnp.where(sel, pos, NOT_SELECTED)
        base_ref[x] = base.astype(jnp.int32)
        cnt_ref[x] = tile_cnt


def select_slots(aff_t, cap):
    e, t = aff_t.shape
    nc = t // LANES
    per = MOE_TILE // LANES
    slot, base, cnt = pl.pallas_call(
        functools.partial(_select_body, cap=cap),
        out_shape=[jax.ShapeDtypeStruct((e, nc, LANES), jnp.int32)] * 3,
        compiler_params=pltpu.CompilerParams(vmem_limit_bytes=VMEM_LIMIT_BYTES), name="select_slots",
    )(aff_t.reshape(e, nc, LANES))
    return slot.reshape(e, t), base[:, ::per, 0].reshape(-1), cnt[:, ::per, 0].reshape(-1)


def _one_hot_t(slot_ref, start, width, lo=None, hi=None):
    tm = slot_ref.shape[1]
    w = lax.broadcasted_iota(jnp.int32, (width, tm), 0)
    rows = []
    for e in range(N_EXPERTS):
        s = slot_ref[e:e + 1, :]
        hit = (s - start[e]) == w
        if lo is not None:
            hit = hit & (s >= lo[e])
        if hi is not None:
            hit = hit & (s < hi[e])
        rows.append(hit)
    return jnp.concatenate(rows, axis=0)


def _tile_scalars(base_ref, cnt_ref, i, nt):
    lo = [base_ref[e * nt + i] for e in range(N_EXPERTS)]
    hi = [lo[e] + cnt_ref[e * nt + i] for e in range(N_EXPERTS)]
    start = [(l // SLAB_ALIGN) * SLAB_ALIGN for l in lo]
    return start, hi


def _rounds(start, hi):
    m = hi[0] - start[0]
    for e in range(1, N_EXPERTS):
        m = jnp.maximum(m, hi[e] - start[e])
    return (m + MOE_SLAB - 1) // MOE_SLAB


def _dispatch_body(base_ref, cnt_ref, x_ref, g_ref, slot_ref, aff_ref, xe_ref, ge_ref,
                   xbuf, gbuf, xover, gover, xcarry, gcarry, sem, osem, *, nt, cap):
    i = pl.program_id(0)
    cur = i % 2
    w_ = MOE_SLAB
    al = SLAB_ALIGN
    hn = _rms(x_ref[...], g_ref[...]).astype(BF16)
    start, hi = _tile_scalars(base_ref, cnt_ref, i, nt)
    nxt = [(h // al) * al for h in hi]

    @pl.when(i == 0)
    def _():
        xcarry[...] = jnp.zeros_like(xcarry)
        gcarry[...] = jnp.zeros_like(gcarry)

    def gather(hot):
        rows = jnp.dot(hot.astype(BF16), hn, preferred_element_type=F32)
        hot_f = hot.astype(F32)
        n = hot.shape[0] // N_EXPERTS
        gates = [jnp.sum(hot_f[e * n:(e + 1) * n] * aff_ref[e:e + 1, :], axis=1, keepdims=True)
                 for e in range(N_EXPERTS)]
        return rows, gates

    def copies(xsrc, gsrc, st, e, xs, gs):
        dst = pl.ds(pl.multiple_of(st[e], al), w_)
        return (pltpu.make_async_copy(xsrc.at[e], xe_ref.at[e, dst], xs),
                pltpu.make_async_copy(gsrc.at[e], ge_ref.at[e, dst], gs))

    rows, gates = gather(_one_hot_t(slot_ref, start, w_))
    keep = []
    for e in range(N_EXPERTS):
        xc = xcarry[e].astype(F32)
        gc = gcarry[e]
        win = rows[e * w_:(e + 1) * w_]
        xbuf[cur, e] = jnp.concatenate([win[:al] + xc, win[al:]], axis=0).astype(BF16)
        gwin = jnp.broadcast_to(gates[e], (w_, LANES))
        gbuf[cur, e] = jnp.concatenate([gwin[:al] + gc, gwin[al:]], axis=0)
        keep.append((xc, gc))
    rows, gates = gather(_one_hot_t(slot_ref, nxt, al))
    for e in range(N_EXPERTS):
        same = (nxt[e] == start[e]).astype(F32)
        xcarry[e] = (rows[e * al:(e + 1) * al] + same * keep[e][0]).astype(BF16)
        gcarry[e] = jnp.broadcast_to(gates[e], (al, LANES)) + same * keep[e][1]

    @pl.when(i > 0)
    def _():
        st_p, _ = _tile_scalars(base_ref, cnt_ref, i - 1, nt)
        for e in range(N_EXPERTS):
            for cp in copies(xbuf.at[1 - cur], gbuf.at[1 - cur], st_p, e, sem.at[1 - cur, 0], sem.at[1 - cur, 1]):
                cp.wait()

    for e in range(N_EXPERTS):
        for prio, cp in enumerate(copies(xbuf.at[cur], gbuf.at[cur], start, e, sem.at[cur, 0], sem.at[cur, 1])):
            cp.start(priority=prio)

    def extra_round(k, carry):
        st_k = [s + k * w_ for s in start]
        rows, gates = gather(_one_hot_t(slot_ref, st_k, w_))
        for e in range(N_EXPERTS):
            xover[e] = rows[e * w_:(e + 1) * w_].astype(BF16)
            gover[e] = jnp.broadcast_to(gates[e], (w_, LANES))
        for e in range(N_EXPERTS):
            @pl.when(hi[e] - start[e] > k * w_)
            def _():
                for cp in copies(xover, gover, st_k, e, osem.at[0], osem.at[1]):
                    cp.start()
        for e in range(N_EXPERTS):
            @pl.when(hi[e] - start[e] > k * w_)
            def _():
                for cp in copies(xover, gover, st_k, e, osem.at[0], osem.at[1]):
                    cp.wait()
        return carry

    lax.fori_loop(1, _rounds(start, hi), extra_round, 0)

    @pl.when(i == nt - 1)
    def _():
        for e in range(N_EXPERTS):
            for cp in copies(xbuf.at[cur], gbuf.at[cur], start, e, sem.at[cur, 0], sem.at[cur, 1]):
                cp.wait()
        for e in range(N_EXPERTS):
            xover[e] = jnp.zeros((w_, xover.shape[-1]), BF16)
            gover[e] = jnp.zeros((w_, LANES), F32)
        tail = [cap] * N_EXPERTS
        for e in range(N_EXPERTS):
            for cp in copies(xover, gover, tail, e, osem.at[0], osem.at[1]):
                cp.start()
        for e in range(N_EXPERTS):
            for cp in copies(xover, gover, tail, e, osem.at[0], osem.at[1]):
                cp.wait()


def dispatch(x, gain, slot, aff_t, base, cnt, cap):
    t, d = x.shape
    tm = MOE_TILE
    nt = t // tm
    rows = cap + MOE_SLAB
    any_spec = pl.BlockSpec(memory_space=pl.ANY)
    grid_spec = pltpu.PrefetchScalarGridSpec(
        num_scalar_prefetch=2, grid=(nt,),
        in_specs=[pl.BlockSpec((tm, d), lambda i, *_: (i, 0)), pl.BlockSpec((1, d), lambda i, *_: (0, 0)),
                  pl.BlockSpec((N_EXPERTS, tm), lambda i, *_: (0, i)),
                  pl.BlockSpec((N_EXPERTS, tm), lambda i, *_: (0, i))],
        out_specs=[any_spec, any_spec],
        scratch_shapes=[pltpu.VMEM((2, N_EXPERTS, MOE_SLAB, d), BF16),
                        pltpu.VMEM((2, N_EXPERTS, MOE_SLAB, LANES), F32),
                        pltpu.VMEM((N_EXPERTS, MOE_SLAB, d), BF16),
                        pltpu.VMEM((N_EXPERTS, MOE_SLAB, LANES), F32),
                        pltpu.VMEM((N_EXPERTS, SLAB_ALIGN, d), BF16),
                        pltpu.VMEM((N_EXPERTS, SLAB_ALIGN, LANES), F32),
                        pltpu.SemaphoreType.DMA((2, 2)), pltpu.SemaphoreType.DMA((2,))])
    return pl.pallas_call(
        functools.partial(_dispatch_body, nt=nt, cap=cap), grid_spec=grid_spec,
        out_shape=[jax.ShapeDtypeStruct((N_EXPERTS, rows, d), BF16),
                   jax.ShapeDtypeStruct((N_EXPERTS, rows, LANES), F32)],
        compiler_params=_params("arbitrary"), name="moe_dispatch",
    )(base, cnt, x, gain.reshape(1, -1), slot, aff_t)


def _ffn_body(xe_ref, gate_ref, wg_ref, wu_ref, wd_ref, o_ref, wg_s, wu_s, wd_s):
    e = pl.program_id(0)
    j = pl.program_id(1)
    rg = wg_ref.shape[2]
    rd = wd_ref.shape[2]

    @pl.when(e < N_EXPERTS)
    def _():
        nxt = e % 2
        wg_s[nxt, pl.ds(pl.multiple_of(j * rg, rg), rg), :] = wg_ref[0, 0].astype(BF16)
        wu_s[nxt, pl.ds(pl.multiple_of(j * rg, rg), rg), :] = wu_ref[0, 0].astype(BF16)
        wd_s[nxt, pl.ds(pl.multiple_of(j * rd, rd), rd), :] = wd_ref[0, 0].astype(BF16)

    @pl.when(e > 0)
    def _():
        cur = (e - 1) % 2
        xe = xe_ref[0]
        a = jnp.dot(xe, wg_s[cur], preferred_element_type=F32)
        b = jnp.dot(xe, wu_s[cur], preferred_element_type=F32)
        hid = (_silu(a) * b).astype(BF16)
        o_ref[0] = (jnp.dot(hid, wd_s[cur], preferred_element_type=F32) * gate_ref[0][:, :1]).astype(o_ref.dtype)


def expert_ffn(xe, gate, wg, wu, wd, layer, cap):
    e, _, d = xe.shape
    f = wg.shape[-1]
    tm = FFN_TILE
    nj = cap // tm
    last = e - 1

    def rows(i, j):
        return (jnp.maximum(i - 1, 0), jnp.where(i == 0, 0, j), 0)

    def chunk(i, j):
        return (layer, jnp.minimum(i, last), jnp.where(i <= last, j, nj - 1), 0)

    return pl.pallas_call(
        _ffn_body, grid=(e + 1, nj),
        in_specs=[
            pl.BlockSpec((1, tm, d), rows),
            pl.BlockSpec((1, tm, LANES), rows),
            pl.BlockSpec((1, 1, d // nj, f), chunk),
            pl.BlockSpec((1, 1, d // nj, f), chunk),
            pl.BlockSpec((1, 1, f // nj, d), chunk),
        ],
        out_specs=pl.BlockSpec((1, tm, d), rows),
        out_shape=jax.ShapeDtypeStruct((e, cap, d), BF16),
        scratch_shapes=[pltpu.VMEM((2, d, f), BF16), pltpu.VMEM((2, d, f), BF16), pltpu.VMEM((2, f, d), BF16)],
        compiler_params=_params("arbitrary", "arbitrary"), name="expert_ffn",
    )(xe, gate, wg, wu, wd)


def _combine_body(base_ref, cnt_ref, x_ref, slot_ref, y_ref, g_ref, o_ref, ybuf, yover, sem, osem,
                  *, nt, cap, out_norm):
    i = pl.program_id(0)
    cur = i % 2
    w_ = MOE_SLAB
    tn = (((0,), (0,)), ((), ()))

    def window(st):
        return [pl.multiple_of(jnp.minimum(s, cap - w_), SLAB_ALIGN) for s in st]

    def fetch(step, buf, xs):
        st, _ = _tile_scalars(base_ref, cnt_ref, step, nt)
        ws = window(st)
        return [pltpu.make_async_copy(y_ref.at[e, pl.ds(ws[e], w_)], buf.at[e], xs)
                for e in range(N_EXPERTS)]

    @pl.when(i == 0)
    def _():
        yover[...] = jnp.zeros_like(yover)
        for cp in fetch(0, ybuf.at[0], sem.at[0]):
            cp.start()

    @pl.when(i + 1 < nt)
    def _():
        for e, cp in enumerate(fetch(i + 1, ybuf.at[1 - cur], sem.at[1 - cur])):
            cp.start(priority=e % 2)

    for cp in fetch(i, ybuf.at[cur], sem.at[cur]):
        cp.wait()

    start, hi = _tile_scalars(base_ref, cnt_ref, i, nt)

    def scatter(buf, st_k, first):
        hot = _one_hot_t(slot_ref, window(st_k), w_, lo=None if first else st_k, hi=[s + w_ for s in st_k])
        ys = buf[...].reshape(N_EXPERTS * w_, buf.shape[-1])
        return lax.dot_general(hot.astype(BF16), ys, tn, preferred_element_type=F32)

    o_ref[...] = x_ref[...] + scatter(ybuf.at[cur], start, True)

    def extra_round(k, carry):
        st_k = [s + k * w_ for s in start]
        ws = window(st_k)
        for e in range(N_EXPERTS):
            @pl.when(hi[e] - start[e] > k * w_)
            def _():
                pltpu.make_async_copy(y_ref.at[e, pl.ds(ws[e], w_)], yover.at[e], osem).start()
        for e in range(N_EXPERTS):
            @pl.when(hi[e] - start[e] > k * w_)
            def _():
                pltpu.make_async_copy(y_ref.at[e, pl.ds(ws[e], w_)], yover.at[e], osem).wait()
        o_ref[...] += scatter(yover, st_k, False)
        return carry

    lax.fori_loop(1, _rounds(start, hi), extra_round, 0)
    if out_norm:
        o_ref[...] = _rms(o_ref[...], g_ref[...])


def combine(x, slot, y, base, cnt, cap, out_gain, out_norm):
    t, d = x.shape
    tm = MOE_TILE
    nt = t // tm
    grid_spec = pltpu.PrefetchScalarGridSpec(
        num_scalar_prefetch=2, grid=(nt,),
        in_specs=[pl.BlockSpec((tm, d), lambda i, *_: (i, 0)),
                  pl.BlockSpec((N_EXPERTS, tm), lambda i, *_: (0, i)),
                  pl.BlockSpec(memory_space=pl.ANY),
                  pl.BlockSpec((1, d), lambda i, *_: (0, 0))],
        out_specs=pl.BlockSpec((tm, d), lambda i, *_: (i, 0)),
        scratch_shapes=[pltpu.VMEM((2, N_EXPERTS, MOE_SLAB, d), BF16),
                        pltpu.VMEM((N_EXPERTS, MOE_SLAB, d), BF16),
                        pltpu.SemaphoreType.DMA((2,)), pltpu.SemaphoreType.DMA(())])
    return pl.pallas_call(
        functools.partial(_combine_body, nt=nt, cap=cap, out_norm=out_norm), grid_spec=grid_spec,
        out_shape=jax.ShapeDtypeStruct((t, d), F32),
        compiler_params=_params("arbitrary"), name="moe_combine",
    )(base, cnt, x, slot, y, out_gain.reshape(1, -1))


def expert_choice_ffn(x, layer, p):
    t, _ = x.shape
    gain = p['norm_ffn'][layer]
    cap = EC_CAPACITY * t // N_EXPERTS
    aff_t = router_affinity(x, gain, p['router'][layer])
    slot, base, cnt = select_slots(aff_t, cap)
    xe, ge = dispatch(x, gain, slot, aff_t, base, cnt, cap)
    y = expert_ffn(xe, ge, p['w_gate'], p['w_up'], p['w_down'], layer, cap)
    return combine(x, slot, y, base, cnt, cap, p['final_norm'], layer == DEPTH - 1)


def trunk(x, mem, p):
    b, n, d = x.shape
    x = x.reshape(b * n, d)
    for layer in range(DEPTH):
        i = layer // 2
        if layer % 2 == 0:
            za, zb = in_proj_ab(x, p['norm_mix'][layer], p['w_in_ab'][i], p['gla_gw_f'][i],
                                p['gla_gb_f'][i], p['gla_gw_b'][i], p['gla_gb_b'][i])
            o_a = neighbourhood_attention(za, p['na_rpb'][i], b, n)
            o_fw, o_bw = gla_bidirectional(zb, b, n)
            x = out_proj_ab(o_a, o_fw, o_bw, zb, p['gla_norm'][i], p['w_out_ab'][i], x)
        else:
            bg, u = in_proj_c(x, p['norm_mix'][layer], p['w_in_c'][i])
            x = conv_out_proj(u, bg, p['conv_w'][i], p['w_out_c'][i], x, n)
        x = memory_attention(x, mem, p['norm_mem'][layer], p['w_mq'][layer], p['w_mk'][layer],
                             p['w_mv'][layer], p['w_mo'][layer], n)
        x = expert_choice_ffn(x, layer, p)
    return x.reshape(b, n, d)


def kernel(x_prompt, x_sample, mem_prompt, mem_sample, w_in_ab, na_rpb, gla_gw_f, gla_gb_f, gla_gw_b,
           gla_gb_b, gla_norm, w_out_ab, w_in_c, conv_w, w_out_c, norm_mix, norm_mem, norm_ffn,
           w_mq, w_mk, w_mv, w_mo, router, w_gate, w_up, w_down, final_norm):
    p = dict(w_in_ab=w_in_ab, na_rpb=na_rpb, gla_gw_f=gla_gw_f, gla_gb_f=gla_gb_f, gla_gw_b=gla_gw_b,
             gla_gb_b=gla_gb_b, gla_norm=gla_norm, w_out_ab=w_out_ab, w_in_c=w_in_c, conv_w=conv_w,
             w_out_c=w_out_c, norm_mix=norm_mix, norm_mem=norm_mem, norm_ffn=norm_ffn, w_mq=w_mq,
             w_mk=w_mk, w_mv=w_mv, w_mo=w_mo, router=router, w_gate=w_gate, w_up=w_up, w_down=w_down,
             final_norm=final_norm)
    y_prompt = trunk(x_prompt, mem_prompt, p)
    y_sample = trunk(x_sample, mem_sample, p)
    return (y_prompt, y_sample)
```
